```python
import jax, jax.numpy as jnp
from jax import lax
import numpy as np

D_MODEL = 4096
BATCH = 2
SEQ = 8192
DEPTH = 2

HEAD_DIM = 128
MOBA_HEADS = 16
SB_HEADS = 16
MOBA_WIDTH = MOBA_HEADS * HEAD_DIM
SB_WIDTH = SB_HEADS * HEAD_DIM
MOBA_BLOCK = 256
MOBA_TOPK = 3
MOBA_Q_CHUNK = 32
SB_Q_BLOCK = 128
D_FF = 4 * D_MODEL
N_BRANCHES = 2
IN_COLS = 3 * MOBA_WIDTH + 3 * SB_WIDTH + N_BRANCHES * D_MODEL
RMS_EPS = 1e-6
NEG = -1e30

kernel_name = "moba_stickbreaking_gated_hybrid"


def rms_norm(x, g):
    xf = x.astype(jnp.float32)
    y = xf * lax.rsqrt(jnp.mean(xf * xf, axis=-1, keepdims=True) + RMS_EPS)
    return (y * g.astype(jnp.float32)).astype(x.dtype)


def alibi_slopes(n_heads):
    return jnp.exp2(-8.0 * jnp.arange(1, n_heads + 1, dtype=jnp.float32) / n_heads)


def to_heads(t, n_heads):
    b, s, _ = t.shape
    return t.reshape(b, s, n_heads, HEAD_DIM).transpose(0, 2, 1, 3)


def from_heads(t):
    b, h, s, d = t.shape
    return t.transpose(0, 2, 1, 3).reshape(b, s, h * d)


def moba_attention(q, k, v, slopes):
    B, H, S, Dh = q.shape
    nb = -(-S // MOBA_BLOCK)
    s_pad = nb * MOBA_BLOCK
    pad = ((0, 0), (0, 0), (0, s_pad - S), (0, 0))
    kp = jnp.pad(k, pad)
    vp = jnp.pad(v, pad)
    k_blocks = kp.reshape(B, H, nb, MOBA_BLOCK, Dh)
    v_blocks = vp.reshape(B, H, nb, MOBA_BLOCK, Dh)
    k_mean = jnp.mean(k_blocks.astype(jnp.float32), axis=3)
    topk = min(MOBA_TOPK, nb)
    scale = Dh ** -0.5
    n_chunks = S // MOBA_Q_CHUNK
    q_chunks = q.reshape(B, H, n_chunks, MOBA_Q_CHUNK, Dh).transpose(2, 0, 1, 3, 4)
    b_idx = jnp.arange(B)[:, None, None, None]
    h_idx = jnp.arange(H)[None, :, None, None]
    blk_pos = jnp.arange(MOBA_BLOCK)
    block_ids = jnp.arange(nb)
    sl5 = slopes[:, None, None, None]
    sl4 = slopes[:, None, None]

    def one_chunk(args):
        c, qc = args
        t = c * MOBA_Q_CHUNK + jnp.arange(MOBA_Q_CHUNK)
        own = (c * MOBA_Q_CHUNK) // MOBA_BLOCK
        gate = jnp.einsum('bhqd,bhnd->bhqn', qc.astype(jnp.float32), k_mean)
        gate = jnp.where(block_ids < own, gate, NEG)
        _, sel = lax.top_k(gate, topk)
        sel_valid = sel < own
        k_sel = k_blocks[b_idx, h_idx, sel]
        v_sel = v_blocks[b_idx, h_idx, sel]
        s_sel = jnp.einsum('bhqd,bhqrkd->bhqrk', qc, k_sel).astype(jnp.float32) * scale
        pos_sel = sel[..., None] * MOBA_BLOCK + blk_pos
        s_sel = s_sel - sl5 * (t[:, None, None] - pos_sel)
        s_sel = jnp.where(sel_valid[..., None], s_sel, NEG)
        k_own = lax.dynamic_slice_in_dim(kp, own * MOBA_BLOCK, MOBA_BLOCK, axis=2)
        v_own = lax.dynamic_slice_in_dim(vp, own * MOBA_BLOCK, MOBA_BLOCK, axis=2)
        dist = t[:, None] - (own * MOBA_BLOCK + blk_pos)[None, :]
        s_own = jnp.einsum('bhqd,bhkd->bhqk', qc, k_own).astype(jnp.float32) * scale
        s_own = jnp.where(dist >= 0, s_own - sl4 * dist, NEG)
        scores = jnp.concatenate([s_sel.reshape(B, H, MOBA_Q_CHUNK, topk * MOBA_BLOCK), s_own], axis=-1)
        p = jax.nn.softmax(scores, axis=-1)
        p_sel = p[..., :topk * MOBA_BLOCK].reshape(B, H, MOBA_Q_CHUNK, topk, MOBA_BLOCK).astype(v.dtype)
        p_own = p[..., topk * MOBA_BLOCK:].astype(v.dtype)
        return (jnp.einsum('bhqrk,bhqrkd->bhqd', p_sel, v_sel)
                + jnp.einsum('bhqk,bhkd->bhqd', p_own, v_own))

    out = lax.map(one_chunk, (jnp.arange(n_chunks), q_chunks))
    return out.transpose(1, 2, 0, 3, 4).reshape(B, H, S, Dh)


def stick_breaking_attention(q, k, v):
    B, H, S, Dh = q.shape
    scale = Dh ** -0.5
    nq = S // SB_Q_BLOCK
    q_blocks = q.reshape(B, H, nq, SB_Q_BLOCK, Dh).transpose(2, 0, 1, 3, 4)
    key_pos = jnp.arange(S)

    def one_block(args):
        c, qb = args
        t = c * SB_Q_BLOCK + jnp.arange(SB_Q_BLOCK)
        z = jnp.einsum('bhqd,bhkd->bhqk', qb, k).astype(jnp.float32) * scale
        strict = key_pos[None, :] < t[:, None]
        log_beta = jax.nn.log_sigmoid(z)
        log_1m = jnp.where(strict, jax.nn.log_sigmoid(-z), 0.0)
        later = lax.cumsum(log_1m, axis=3, reverse=True) - log_1m
        w = jnp.where(strict, jnp.exp(log_beta + later), 0.0)
        return jnp.einsum('bhqk,bhkd->bhqd', w.astype(v.dtype), v)

    out = lax.map(one_block, (jnp.arange(nq), q_blocks))
    return out.transpose(1, 2, 0, 3, 4).reshape(B, H, S, Dh)


def hybrid_layer(x, g_mix, w_in, b_gate, g_q, g_k, w_br_moba, w_br_sb, w_out, g_mlp, w_up, w_down, slopes):
    h = rms_norm(x, g_mix)
    proj = h @ w_in
    o = np.cumsum([0, MOBA_WIDTH, MOBA_WIDTH, MOBA_WIDTH, SB_WIDTH, SB_WIDTH, SB_WIDTH, D_MODEL, D_MODEL])
    qa, ka, va, qb, kb, vb, ga, gb = [proj[..., o[i]:o[i + 1]] for i in range(8)]
    qa = rms_norm(to_heads(qa, MOBA_HEADS), g_q)
    ka = rms_norm(to_heads(ka, MOBA_HEADS), g_k)
    ya = from_heads(moba_attention(qa, ka, to_heads(va, MOBA_HEADS), slopes))
    yb = from_heads(stick_breaking_attention(to_heads(qb, SB_HEADS), to_heads(kb, SB_HEADS),
                                             to_heads(vb, SB_HEADS)))
    gates = jax.nn.sigmoid((jnp.concatenate([ga, gb], axis=-1) + b_gate).astype(jnp.float32)).astype(x.dtype)
    merged = gates[..., :D_MODEL] * (ya @ w_br_moba) + gates[..., D_MODEL:] * (yb @ w_br_sb)
    x = x + merged @ w_out
    h2 = rms_norm(x, g_mlp)
    return x + jnp.square(jax.nn.relu(h2 @ w_up)) @ w_down


def setup_inputs(seed: int = 0) -> dict:
    key = jax.random.key(seed)
    ks = jax.random.split(key, 12)
    f32 = jnp.float32
    nrm = lambda k, shape, s: jax.random.normal(k, shape, f32) * s
    return {
        "x": nrm(ks[0], (BATCH, SEQ, D_MODEL), 1.0),
        "norm_mix": 1.0 + nrm(ks[1], (DEPTH, D_MODEL), 0.02),
        "w_in": nrm(ks[2], (DEPTH, D_MODEL, IN_COLS), D_MODEL ** -0.5),
        "b_gate": nrm(ks[3], (DEPTH, N_BRANCHES * D_MODEL), 0.02),
        "q_norm": 1.0 + nrm(ks[4], (DEPTH, HEAD_DIM), 0.02),
        "k_norm": 1.0 + nrm(ks[5], (DEPTH, HEAD_DIM), 0.02),
        "w_branch_moba": nrm(ks[6], (DEPTH, MOBA_WIDTH, D_MODEL), MOBA_WIDTH ** -0.5),
        "w_branch_sb": nrm(ks[7], (DEPTH, SB_WIDTH, D_MODEL), SB_WIDTH ** -0.5),
        "w_out": nrm(ks[8], (DEPTH, D_MODEL, D_MODEL), D_MODEL ** -0.5),
        "norm_mlp": 1.0 + nrm(ks[9], (DEPTH, D_MODEL), 0.02),
        "w_up": nrm(ks[10], (DEPTH, D_MODEL, D_FF), D_MODEL ** -0.5),
        "w_down": nrm(ks[11], (DEPTH, D_FF, D_MODEL), D_FF ** -0.5),
    }


def reference(x, norm_mix, w_in, b_gate, q_norm, k_norm, w_branch_moba, w_branch_sb, w_out,
              norm_mlp, w_up, w_down):
    slopes = alibi_slopes(MOBA_HEADS)
    for l in range(DEPTH):
        x = hybrid_layer(x, norm_mix[l], w_in[l], b_gate[l], q_norm[l], k_norm[l], w_branch_moba[l],
                         w_branch_sb[l], w_out[l], norm_mlp[l], w_up[l], w_down[l], slopes)
    return x
```

```python
import functools

import jax
import jax.numpy as jnp
from jax import lax
from jax.experimental import pallas as pl
from jax.experimental.pallas import tpu as pltpu

D_MODEL = 4096
HEAD_DIM = 128
N_HEADS = 16
WIDTH = N_HEADS * HEAD_DIM
MOBA_BLOCK = 256
MOBA_TOPK = 3
D_FF = 4 * D_MODEL
RMS_EPS = 1e-6
NEG = -1e30
SCALE = HEAD_DIM ** -0.5

VMEM_LIMIT_BYTES = 56 * 1024 * 1024

F32 = jnp.float32
BF16 = jnp.bfloat16


def _params(semantics):
    return pltpu.CompilerParams(dimension_semantics=semantics, vmem_limit_bytes=VMEM_LIMIT_BYTES)


def _rms_kernel(x_ref, g_ref, o_ref):
    x = x_ref[...]
    ms = jnp.mean(x * x, axis=-1, keepdims=True)
    o_ref[...] = ((x * lax.rsqrt(ms + RMS_EPS)) * g_ref[...]).astype(o_ref.dtype)


def rms_norm(x, g, *, tm=256):
    m, d = x.shape
    return pl.pallas_call(
        _rms_kernel,
        grid=(m // tm,),
        in_specs=[pl.BlockSpec((tm, d), lambda i: (i, 0)),
                  pl.BlockSpec((1, d), lambda i: (0, 0))],
        out_specs=pl.BlockSpec((tm, d), lambda i: (i, 0)),
        out_shape=jax.ShapeDtypeStruct((m, d), BF16),
        compiler_params=_params(("arbitrary",)),
        name="rms_norm",
    )(x, g.reshape(1, d))


def _mm_plain_kernel(a_ref, w_ref, o_ref):
    o_ref[...] = jnp.dot(a_ref[...], w_ref[...], preferred_element_type=F32).astype(o_ref.dtype)


def _mm_headnorm_kernel(a_ref, w_ref, g_ref, o_ref):
    acc = jnp.dot(a_ref[...], w_ref[...], preferred_element_type=F32)
    for c in range(acc.shape[1] // HEAD_DIM):
        sl = slice(c * HEAD_DIM, (c + 1) * HEAD_DIM)
        y = acc[:, sl]
        ms = jnp.mean(y * y, axis=-1, keepdims=True)
        o_ref[:, sl] = ((y * lax.rsqrt(ms + RMS_EPS)) * g_ref[:, sl]).astype(o_ref.dtype)


def _mm_sigmoid_kernel(a_ref, w_ref, b_ref, o_ref):
    acc = jnp.dot(a_ref[...], w_ref[...], preferred_element_type=F32)
    o_ref[...] = jax.nn.sigmoid(acc + b_ref[...]).astype(o_ref.dtype)


def _mm_relu2_kernel(a_ref, w_ref, o_ref):
    acc = jnp.dot(a_ref[...], w_ref[...], preferred_element_type=F32)
    o_ref[...] = jnp.square(jnp.maximum(acc, 0.0)).astype(o_ref.dtype)


def _mm_residual_kernel(a_ref, w_ref, r_ref, o_ref):
    acc = jnp.dot(a_ref[...], w_ref[...], preferred_element_type=F32)
    o_ref[...] = r_ref[...] + acc


def matmul(kernel, a, w, extras, *, n, col_off, out_dtype, tm, tn, name):
    m, k = a.shape
    off = col_off // tn
    assert off * tn == col_off and n % tn == 0 and m % tm == 0
    in_specs = [pl.BlockSpec((tm, k), lambda i, j: (i, 0)),
                pl.BlockSpec((k, tn), lambda i, j: (0, j + off))]
    args = [a, w]
    for arr, kind in extras:
        if kind == "row":
            in_specs.append(pl.BlockSpec((1, tn), lambda i, j: (0, j)))
        else:
            in_specs.append(pl.BlockSpec((tm, tn), lambda i, j: (i, j)))
        args.append(arr)
    return pl.pallas_call(
        kernel,
        grid=(m // tm, n // tn),
        in_specs=in_specs,
        out_specs=pl.BlockSpec((tm, tn), lambda i, j: (i, j)),
        out_shape=jax.ShapeDtypeStruct((m, n), out_dtype),
        compiler_params=_params(("arbitrary", "arbitrary")),
        name=name,
    )(*args)


def _mm_kgrid_residual_kernel(a_ref, w_ref, r_ref, o_ref, acc_ref):
    kk = pl.program_id(2)

    @pl.when(kk == 0)
    def _():
        acc_ref[...] = jnp.zeros_like(acc_ref)

    acc_ref[...] += jnp.dot(a_ref[...], w_ref[...], preferred_element_type=F32)

    @pl.when(kk == pl.num_programs(2) - 1)
    def _():
        o_ref[...] = r_ref[...] + acc_ref[...]


def matmul_kgrid_residual(a, w, r, *, tm, tn, tk, name):
    m, k = a.shape
    n = w.shape[1]
    return pl.pallas_call(
        _mm_kgrid_residual_kernel,
        grid=(m // tm, n // tn, k // tk),
        in_specs=[pl.BlockSpec((tm, tk), lambda i, j, kk: (i, kk)),
                  pl.BlockSpec((tk, tn), lambda i, j, kk: (kk, j)),
                  pl.BlockSpec((tm, tn), lambda i, j, kk: (i, j))],
        out_specs=pl.BlockSpec((tm, tn), lambda i, j, kk: (i, j)),
        out_shape=jax.ShapeDtypeStruct((m, n), F32),
        scratch_shapes=[pltpu.VMEM((tm, tn), F32)],
        compiler_params=_params(("arbitrary", "arbitrary", "arbitrary")),
        name=name,
    )(a, w, r)


def _merge_kernel(ya_ref, yb_ref, wa_ref, wb_ref, ga_ref, gb_ref, o_ref):
    pa = jnp.dot(ya_ref[...], wa_ref[...], preferred_element_type=F32)
    pb = jnp.dot(yb_ref[...], wb_ref[...], preferred_element_type=F32)
    o_ref[...] = (ga_ref[...] * pa + gb_ref[...] * pb).astype(o_ref.dtype)


def merge_branches(ya, yb, wa, wb, gates, *, tm=1024, tn=512):
    m, k = ya.shape
    n = wa.shape[1]
    goff = n // tn
    return pl.pallas_call(
        _merge_kernel,
        grid=(m // tm, n // tn),
        in_specs=[pl.BlockSpec((tm, k), lambda i, j: (i, 0)),
                  pl.BlockSpec((tm, k), lambda i, j: (i, 0)),
                  pl.BlockSpec((k, tn), lambda i, j: (0, j)),
                  pl.BlockSpec((k, tn), lambda i, j: (0, j)),
                  pl.BlockSpec((tm, tn), lambda i, j: (i, j)),
                  pl.BlockSpec((tm, tn), lambda i, j: (i, j + goff))],
        out_specs=pl.BlockSpec((tm, tn), lambda i, j: (i, j)),
        out_shape=jax.ShapeDtypeStruct((m, n), BF16),
        compiler_params=_params(("arbitrary", "arbitrary")),
        name="merge_branches",
    )(ya, yb, wa, wb, gates, gates)


ATT_TILE = 256


def _transpose_values(v_ref, vt_ref):
    n_chunks = v_ref.shape[0] // ATT_TILE

    def body(c, carry):
        r0 = pl.multiple_of(c * ATT_TILE, ATT_TILE)
        vt_ref[:, pl.ds(r0, ATT_TILE)] = v_ref[pl.ds(r0, ATT_TILE), :].astype(F32).T.astype(BF16)
        return carry

    lax.fori_loop(0, n_chunks, body, 0)


def _nt_dot(a, b):
    return lax.dot_general(a, b, (((1,), (1,)), ((), ())), preferred_element_type=F32)


def _sb_kernel(q_ref, k_ref, v_ref, o_ref, vt_ref, acc_ref):
    i = pl.program_id(2)
    t = ATT_TILE

    @pl.when(i == 0)
    def _():
        _transpose_values(v_ref, vt_ref)

    q = q_ref[...]
    row = lax.broadcasted_iota(jnp.int32, (t, t), 0)
    col = lax.broadcasted_iota(jnp.int32, (t, t), 1)
    upper = (col > row).astype(BF16)

    def block(j, carry, diagonal):
        k0 = pl.multiple_of(j * t, t)
        z = _nt_dot(k_ref[pl.ds(k0, t), :], q) * SCALE
        log_beta = jnp.minimum(z, 0.0) - jnp.log(1.0 + jnp.exp(-jnp.abs(z)))
        log_1m = log_beta - z
        if diagonal:
            strict = row < col
            log_1m = jnp.where(strict, log_1m, 0.0)
        hi = log_1m.astype(BF16)
        lo = (log_1m - hi.astype(F32)).astype(BF16)
        later = (jnp.dot(upper, hi, preferred_element_type=F32)
                 + jnp.dot(upper, lo, preferred_element_type=F32)) + carry
        w = jnp.exp(log_beta + later)
        if diagonal:
            w = jnp.where(strict, w, 0.0)
        pv = jnp.dot(vt_ref[:, pl.ds(k0, t)], w.astype(BF16), preferred_element_type=F32)
        return carry + jnp.sum(log_1m, axis=0, keepdims=True), pv

    carry, pv = block(i, jnp.zeros((1, t), F32), True)
    acc_ref[...] = pv

    def body(step, carry):
        carry, pv = block(i - 1 - step, carry, False)
        acc_ref[...] += pv
        return carry

    lax.fori_loop(0, i, body, carry)
    o_ref[...] = acc_ref[...].T.astype(o_ref.dtype)


def sb_attention(p2, batch, seq):
    t = ATT_TILE
    nq = seq // t
    qc, kc, vc = WIDTH // HEAD_DIM, 2 * WIDTH // HEAD_DIM, 3 * WIDTH // HEAD_DIM
    return pl.pallas_call(
        _sb_kernel,
        grid=(batch, N_HEADS, nq),
        in_specs=[pl.BlockSpec((t, HEAD_DIM), lambda b, h, i: (b * nq + i, qc + h)),
                  pl.BlockSpec((seq, HEAD_DIM), lambda b, h, i: (b, kc + h)),
                  pl.BlockSpec((seq, HEAD_DIM), lambda b, h, i: (b, vc + h))],
        out_specs=pl.BlockSpec((t, HEAD_DIM), lambda b, h, i: (b * nq + i, h)),
        out_shape=jax.ShapeDtypeStruct((batch * seq, WIDTH), BF16),
        scratch_shapes=[pltpu.VMEM((HEAD_DIM, seq), BF16),
                        pltpu.VMEM((HEAD_DIM, t), F32)],
        compiler_params=_params(("arbitrary", "arbitrary", "arbitrary")),
        name="sb_attention",
    )(p2, p2, p2)


def _moba_kernel(slopes_ref, q_ref, k_ref, v_ref, o_ref, vt_ref, kmh_ref, kml_ref, sel_ref,
                 acc_ref, *, n_blocks):
    h = pl.program_id(1)
    i = pl.program_id(2)
    t = ATT_TILE
    slope = slopes_ref[h]

    @pl.when(i == 0)
    def _():
        _transpose_values(v_ref, vt_ref)
        km = jnp.mean(k_ref[...].astype(F32).reshape(n_blocks, t, HEAD_DIM), axis=1)
        hi = km.astype(BF16)
        kmh_ref[...] = hi
        kml_ref[...] = (km - hi.astype(F32)).astype(BF16)

    q = q_ref[...]

    gate = _nt_dot(kmh_ref[...], q) + _nt_dot(kml_ref[...], q)
    blk = lax.broadcasted_iota(jnp.int32, (n_blocks, t), 0)
    valid = blk < i
    gate = jnp.where(valid, gate, NEG)
    chosen = jnp.zeros((n_blocks, t), jnp.bool_)
    for _ in range(MOBA_TOPK):
        top = jnp.max(gate, axis=0, keepdims=True)
        first = jnp.min(jnp.where(gate == top, blk, n_blocks), axis=0, keepdims=True)
        pick = blk == first
        chosen = jnp.logical_or(chosen, pick)
        gate = jnp.where(pick, -jnp.inf, gate)
    sel_ref[...] = jnp.logical_and(chosen, valid).astype(F32)

    row = lax.broadcasted_iota(jnp.int32, (t, t), 0)
    col = lax.broadcasted_iota(jnp.int32, (t, t), 1)
    dist0 = (col - row).astype(F32)

    s = _nt_dot(k_ref[pl.ds(pl.multiple_of(i * t, t), t), :], q) * SCALE
    s = jnp.where(col >= row, s - slope * dist0, NEG)
    m = jnp.max(s, axis=0, keepdims=True)
    p = jnp.exp(s - m)
    l = jnp.sum(p, axis=0, keepdims=True)
    acc_ref[...] = jnp.dot(vt_ref[:, pl.ds(pl.multiple_of(i * t, t), t)], p.astype(BF16),
                           preferred_element_type=F32)

    def body(j, carry):
        m, l = carry
        k0 = pl.multiple_of(j * t, t)
        s = _nt_dot(k_ref[pl.ds(k0, t), :], q) * SCALE
        s = s - slope * (dist0 + ((i - j) * t).astype(F32))
        s = jnp.where(sel_ref[pl.ds(j, 1), :] > 0.0, s, NEG)
        m_new = jnp.maximum(m, jnp.max(s, axis=0, keepdims=True))
        alpha = jnp.exp(m - m_new)
        p = jnp.exp(s - m_new)
        l = alpha * l + jnp.sum(p, axis=0, keepdims=True)
        acc_ref[...] = alpha * acc_ref[...] + jnp.dot(
            vt_ref[:, pl.ds(k0, t)], p.astype(BF16), preferred_element_type=F32)
        return m_new, l

    m, l = lax.fori_loop(0, i, body, (m, l))
    o_ref[...] = (acc_ref[...] / l).T.astype(o_ref.dtype)


def moba_attention(qk, p2, slopes, batch, seq):
    t = ATT_TILE
    nq = seq // t
    n_blocks = seq // MOBA_BLOCK
    kc = WIDTH // HEAD_DIM
    grid_spec = pltpu.PrefetchScalarGridSpec(
        num_scalar_prefetch=1,
        grid=(batch, N_HEADS, nq),
        in_specs=[pl.BlockSpec((t, HEAD_DIM), lambda b, h, i, s: (b * nq + i, h)),
                  pl.BlockSpec((seq, HEAD_DIM), lambda b, h, i, s: (b, kc + h)),
                  pl.BlockSpec((seq, HEAD_DIM), lambda b, h, i, s: (b, h))],
        out_specs=pl.BlockSpec((t, HEAD_DIM), lambda b, h, i, s: (b * nq + i, h)),
        scratch_shapes=[pltpu.VMEM((HEAD_DIM, seq), BF16),
                        pltpu.VMEM((n_blocks, HEAD_DIM), BF16),
                        pltpu.VMEM((n_blocks, HEAD_DIM), BF16),
                        pltpu.VMEM((n_blocks, t), F32),
                        pltpu.VMEM((HEAD_DIM, t), F32)],
    )
    return pl.pallas_call(
        functools.partial(_moba_kernel, n_blocks=n_blocks),
        grid_spec=grid_spec,
        out_shape=jax.ShapeDtypeStruct((batch * seq, WIDTH), BF16),
        compiler_params=_params(("arbitrary", "arbitrary", "arbitrary")),
        name="moba_attention",
    )(slopes, qk, qk, p2)


def _layer(x, g_mix, w_in, b_gate, g_q, g_k, w_bm, w_bs, w_out, g_mlp, w_up, w_down, slopes,
           batch, seq):
    h = rms_norm(x, g_mix)
    qk_gain = jnp.concatenate([jnp.tile(g_q, N_HEADS), jnp.tile(g_k, N_HEADS)]).reshape(1, 2 * WIDTH)
    qk = matmul(_mm_headnorm_kernel, h, w_in, [(qk_gain, "row")], n=2 * WIDTH, col_off=0,
                out_dtype=BF16, tm=1024, tn=1024, name="in_proj_qk_moba")
    p2 = matmul(_mm_plain_kernel, h, w_in, [], n=4 * WIDTH, col_off=2 * WIDTH,
                out_dtype=BF16, tm=1024, tn=1024, name="in_proj_rest")
    gates = matmul(_mm_sigmoid_kernel, h, w_in, [(b_gate.reshape(1, -1), "row")], n=2 * D_MODEL,
                   col_off=6 * WIDTH, out_dtype=F32, tm=1024, tn=1024, name="in_proj_gates")
    ya = moba_attention(qk, p2, slopes, batch, seq)
    yb = sb_attention(p2, batch, seq)
    merged = merge_branches(ya, yb, w_bm, w_bs, gates)
    x = matmul(_mm_residual_kernel, merged, w_out, [(x, "tile")], n=D_MODEL, col_off=0,
               out_dtype=F32, tm=1024, tn=512, name="out_proj")
    h2 = rms_norm(x, g_mlp)
    u = matmul(_mm_relu2_kernel, h2, w_up, [], n=D_FF, col_off=0,
               out_dtype=BF16, tm=1024, tn=1024, name="mlp_up")
    return matmul_kgrid_residual(u, w_down, x, tm=1024, tn=1024, tk=2048, name="mlp_down")


def kernel(x, norm_mix, w_in, b_gate, q_norm, k_norm, w_branch_moba, w_branch_sb, w_out,
           norm_mlp, w_up, w_down):
    batch, seq, d = x.shape
    depth = w_in.shape[0]
    slopes = jnp.exp2(-8.0 * jnp.arange(1, N_HEADS + 1, dtype=F32) / N_HEADS)
    y = x.reshape(batch * seq, d)
    for l in range(depth):
        y = _layer(y, norm_mix[l], w_in[l].astype(BF16), b_gate[l], q_norm[l], k_norm[l],
                   w_branch_moba[l].astype(BF16), w_branch_sb[l].astype(BF16),
                   w_out[l].astype(BF16), norm_mlp[l], w_up[l].astype(BF16),
                   w_down[l].astype(BF16), slopes, batch, seq)
    return y.reshape(batch, seq, d)
```

```python
import functools

import jax
import jax.numpy as jnp
from jax import lax
from jax.experimental import pallas as pl
from jax.experimental.pallas import tpu as pltpu

D_MODEL = 4096
HEAD_DIM = 128
N_HEADS = 16
WIDTH = N_HEADS * HEAD_DIM
MOBA_BLOCK = 256
MOBA_TOPK = 3
D_FF = 4 * D_MODEL
RMS_EPS = 1e-6
NEG = -1e30
SCALE = HEAD_DIM ** -0.5

VMEM_LIMIT_BYTES = 56 * 1024 * 1024

F32 = jnp.float32
BF16 = jnp.bfloat16


def _params(semantics):
    return pltpu.CompilerParams(dimension_semantics=semantics, vmem_limit_bytes=VMEM_LIMIT_BYTES)


def _rms_kernel(x_ref, g_ref, o_ref):
    x = x_ref[...]
    ms = jnp.mean(x * x, axis=-1, keepdims=True)
    o_ref[...] = ((x * lax.rsqrt(ms + RMS_EPS)) * g_ref[...]).astype(o_ref.dtype)


def rms_norm(x, g, *, tm=256):
    m, d = x.shape
    return pl.pallas_call(
        _rms_kernel,
        grid=(m // tm,),
        in_specs=[pl.BlockSpec((tm, d), lambda i: (i, 0)),
                  pl.BlockSpec((1, d), lambda i: (0, 0))],
        out_specs=pl.BlockSpec((tm, d), lambda i: (i, 0)),
        out_shape=jax.ShapeDtypeStruct((m, d), BF16),
        compiler_params=_params(("arbitrary",)),
        name="rms_norm",
    )(x, g.reshape(1, d))


def _mm_plain_kernel(a_ref, w_ref, o_ref):
    o_ref[...] = jnp.dot(a_ref[...], w_ref[...], preferred_element_type=F32).astype(o_ref.dtype)


def _mm_headnorm_kernel(a_ref, w_ref, g_ref, o_ref):
    acc = jnp.dot(a_ref[...], w_ref[...], preferred_element_type=F32)
    for c in range(acc.shape[1] // HEAD_DIM):
        sl = slice(c * HEAD_DIM, (c + 1) * HEAD_DIM)
        y = acc[:, sl]
        ms = jnp.mean(y * y, axis=-1, keepdims=True)
        o_ref[:, sl] = ((y * lax.rsqrt(ms + RMS_EPS)) * g_ref[:, sl]).astype(o_ref.dtype)


def _mm_sigmoid_kernel(a_ref, w_ref, b_ref, o_ref):
    acc = jnp.dot(a_ref[...], w_ref[...], preferred_element_type=F32)
    o_ref[...] = jax.nn.sigmoid(acc + b_ref[...]).astype(o_ref.dtype)


def _mm_relu2_kernel(a_ref, w_ref, o_ref):
    acc = jnp.dot(a_ref[...], w_ref[...], preferred_element_type=F32)
    o_ref[...] = jnp.square(jnp.maximum(acc, 0.0)).astype(o_ref.dtype)


def _mm_residual_kernel(a_ref, w_ref, r_ref, o_ref):
    acc = jnp.dot(a_ref[...], w_ref[...], preferred_element_type=F32)
    o_ref[...] = r_ref[...] + acc


def matmul(kernel, a, w, extras, *, n, col_off, out_dtype, tm, tn, name):
    m, k = a.shape
    off = col_off // tn
    assert off * tn == col_off and n % tn == 0 and m % tm == 0
    in_specs = [pl.BlockSpec((tm, k), lambda i, j: (i, 0)),
                pl.BlockSpec((k, tn), lambda i, j: (0, j + off))]
    args = [a, w]
    for arr, kind in extras:
        if kind == "row":
            in_specs.append(pl.BlockSpec((1, tn), lambda i, j: (0, j)))
        else:
            in_specs.append(pl.BlockSpec((tm, tn), lambda i, j: (i, j)))
        args.append(arr)
    return pl.pallas_call(
        kernel,
        grid=(m // tm, n // tn),
        in_specs=in_specs,
        out_specs=pl.BlockSpec((tm, tn), lambda i, j: (i, j)),
        out_shape=jax.ShapeDtypeStruct((m, n), out_dtype),
        compiler_params=_params(("arbitrary", "arbitrary")),
        name=name,
    )(*args)


def _mm_kgrid_residual_kernel(a_ref, w_ref, r_ref, o_ref, acc_ref):
    kk = pl.program_id(2)

    @pl.when(kk == 0)
    def _():
        acc_ref[...] = jnp.zeros_like(acc_ref)

    acc_ref[...] += jnp.dot(a_ref[...], w_ref[...], preferred_element_type=F32)

    @pl.when(kk == pl.num_programs(2) - 1)
    def _():
        o_ref[...] = r_ref[...] + acc_ref[...]


def matmul_kgrid_residual(a, w, r, *, tm, tn, tk, name):
    m, k = a.shape
    n = w.shape[1]
    return pl.pallas_call(
        _mm_kgrid_residual_kernel,
        grid=(m // tm, n // tn, k // tk),
        in_specs=[pl.BlockSpec((tm, tk), lambda i, j, kk: (i, kk)),
                  pl.BlockSpec((tk, tn), lambda i, j, kk: (kk, j)),
                  pl.BlockSpec((tm, tn), lambda i, j, kk: (i, j))],
        out_specs=pl.BlockSpec((tm, tn), lambda i, j, kk: (i, j)),
        out_shape=jax.ShapeDtypeStruct((m, n), F32),
        scratch_shapes=[pltpu.VMEM((tm, tn), F32)],
        compiler_params=_params(("arbitrary", "arbitrary", "arbitrary")),
        name=name,
    )(a, w, r)


def _merge_kernel(ya_ref, yb_ref, wa_ref, wb_ref, ga_ref, gb_ref, o_ref):
    pa = jnp.dot(ya_ref[...], wa_ref[...], preferred_element_type=F32)
    pb = jnp.dot(yb_ref[...], wb_ref[...], preferred_element_type=F32)
    o_ref[...] = (ga_ref[...] * pa + gb_ref[...] * pb).astype(o_ref.dtype)


def merge_branches(ya, yb, wa, wb, gates, *, tm=1024, tn=512):
    m, k = ya.shape
    n = wa.shape[1]
    goff = n // tn
    return pl.pallas_call(
        _merge_kernel,
        grid=(m // tm, n // tn),
        in_specs=[pl.BlockSpec((tm, k), lambda i, j: (i, 0)),
                  pl.BlockSpec((tm, k), lambda i, j: (i, 0)),
                  pl.BlockSpec((k, tn), lambda i, j: (0, j)),
                  pl.BlockSpec((k, tn), lambda i, j: (0, j)),
                  pl.BlockSpec((tm, tn), lambda i, j: (i, j)),
                  pl.BlockSpec((tm, tn), lambda i, j: (i, j + goff))],
        out_specs=pl.BlockSpec((tm, tn), lambda i, j: (i, j)),
        out_shape=jax.ShapeDtypeStruct((m, n), BF16),
        compiler_params=_params(("arbitrary", "arbitrary")),
        name="merge_branches",
    )(ya, yb, wa, wb, gates, gates)


ATT_TILE = 256
HEADS_PER_STEP = 4
GROUP_WIDTH = HEADS_PER_STEP * HEAD_DIM
N_GROUPS = N_HEADS // HEADS_PER_STEP

LOG2E = 1.4426950408889634
INV_LN2 = LOG2E
SCORE_LOG2 = SCALE * LOG2E
SB_EXIT_LOG2 = -160.0


def _head_cols(g):
    return slice(g * HEAD_DIM, (g + 1) * HEAD_DIM)


def _transpose_values(v_ref, vt_ref, g):
    n_chunks = v_ref.shape[0] // ATT_TILE

    def body(c, carry):
        r0 = pl.multiple_of(c * ATT_TILE, ATT_TILE)
        vt_ref[g, :, pl.ds(r0, ATT_TILE)] = (
            v_ref[pl.ds(r0, ATT_TILE), _head_cols(g)].astype(F32).T.astype(BF16))
        return carry

    lax.fori_loop(0, n_chunks, body, 0)


def _nt_dot(a, b):
    return lax.dot_general(a, b, (((1,), (1,)), ((), ())), preferred_element_type=F32)


def _neg_abs(x):
    bits = lax.bitcast_convert_type(x, jnp.uint32) | jnp.uint32(0x80000000)
    return lax.bitcast_convert_type(bits, F32)


def _sb_kernel(q_ref, k_ref, v_ref, o_ref, vt_ref, acc_ref):
    i = pl.program_id(2)
    t = ATT_TILE
    heads = range(HEADS_PER_STEP)

    @pl.when(i == 0)
    def _():
        for g in heads:
            _transpose_values(v_ref, vt_ref, g)

    row = lax.broadcasted_iota(jnp.int32, (t, t), 0)
    col = lax.broadcasted_iota(jnp.int32, (t, t), 1)
    upper = (col > row).astype(BF16)

    def block(j, carries, diagonal):
        k0 = pl.multiple_of(j * t, t)
        zs = [_nt_dot(k_ref[pl.ds(k0, t), _head_cols(g)], q_ref[:, _head_cols(g)]) for g in heads]
        log_betas, log_1ms, cums = [], [], []
        for g in heads:
            z = zs[g] * SCORE_LOG2
            log_beta = jnp.minimum(z, 0.0) - jnp.log(1.0 + jnp.exp2(_neg_abs(z))) * INV_LN2
            log_1m = log_beta - z
            if diagonal:
                log_1m = jnp.where(row < col, log_1m, 0.0)
            hi = log_1m.astype(BF16)
            lo = (log_1m - hi.astype(F32)).astype(BF16)
            cums.append(jnp.dot(upper, hi, preferred_element_type=F32)
                        + jnp.dot(upper, lo, preferred_element_type=F32))
            log_betas.append(log_beta)
            log_1ms.append(log_1m)
        pvs, new_carries = [], []
        for g in heads:
            w = jnp.exp2(log_betas[g] + (cums[g] + carries[g]))
            if diagonal:
                w = jnp.where(row < col, w, 0.0)
            pvs.append(jnp.dot(vt_ref[g, :, pl.ds(k0, t)], w.astype(BF16),
                               preferred_element_type=F32))
            new_carries.append(carries[g] + jnp.sum(log_1ms[g], axis=0, keepdims=True))
        return new_carries, pvs

    def any_alive(carries):
        top = carries[0]
        for c in carries[1:]:
            top = jnp.maximum(top, c)
        return jnp.max(top, axis=1, keepdims=True)[0, 0] > SB_EXIT_LOG2

    carries, pvs = block(i, [jnp.zeros((1, t), F32) for _ in heads], True)
    for g in heads:
        acc_ref[g] = pvs[g]

    def cond(state):
        step, alive = state[0], state[1]
        return jnp.logical_and(step < i, alive)

    def body(state):
        step = state[0]
        carries, pvs = block(i - 1 - step, list(state[2:]), False)
        for g in heads:
            acc_ref[g] += pvs[g]
        return (step + 1, any_alive(carries), *carries)

    lax.while_loop(cond, body, (jnp.int32(0), any_alive(carries), *carries))
    for g in heads:
        o_ref[:, _head_cols(g)] = acc_ref[g].T.astype(o_ref.dtype)


def sb_attention(p2, batch, seq):
    t = ATT_TILE
    nq = seq // t
    qc, kc, vc = N_GROUPS, 2 * N_GROUPS, 3 * N_GROUPS
    return pl.pallas_call(
        _sb_kernel,
        grid=(batch, N_GROUPS, nq),
        in_specs=[pl.BlockSpec((t, GROUP_WIDTH), lambda b, h, i: (b * nq + i, qc + h)),
                  pl.BlockSpec((seq, GROUP_WIDTH), lambda b, h, i: (b, kc + h)),
                  pl.BlockSpec((seq, GROUP_WIDTH), lambda b, h, i: (b, vc + h))],
        out_specs=pl.BlockSpec((t, GROUP_WIDTH), lambda b, h, i: (b * nq + i, h)),
        out_shape=jax.ShapeDtypeStruct((batch * seq, WIDTH), BF16),
        scratch_shapes=[pltpu.VMEM((HEADS_PER_STEP, HEAD_DIM, seq), BF16),
                        pltpu.VMEM((HEADS_PER_STEP, HEAD_DIM, t), F32)],
        compiler_params=_params(("arbitrary", "arbitrary", "arbitrary")),
        name="sb_attention",
    )(p2, p2, p2)


def _moba_kernel(slopes_ref, q_ref, k_ref, v_ref, o_ref, vt_ref, kmh_ref, kml_ref, kbias_ref,
                 selb_ref, acc_ref, *, n_blocks):
    hg = pl.program_id(1)
    i = pl.program_id(2)
    t = ATT_TILE
    heads = range(HEADS_PER_STEP)

    row = lax.broadcasted_iota(jnp.int32, (t, t), 0)
    col = lax.broadcasted_iota(jnp.int32, (t, t), 1)
    slope_log2 = [slopes_ref[hg * HEADS_PER_STEP + g] * LOG2E for g in heads]

    @pl.when(i == 0)
    def _():
        for g in heads:
            _transpose_values(v_ref, vt_ref, g)
            km = jnp.mean(k_ref[:, _head_cols(g)].astype(F32).reshape(n_blocks, t, HEAD_DIM), axis=1)
            hi = km.astype(BF16)
            kmh_ref[g] = hi
            kml_ref[g] = (km - hi.astype(F32)).astype(BF16)
            kbias_ref[g] = row.astype(F32) * slope_log2[g]

    blk = lax.broadcasted_iota(jnp.int32, (n_blocks, t), 0)
    valid = blk < i
    own0 = pl.multiple_of(i * t, t)

    stats = []
    for g in heads:
        q = q_ref[:, _head_cols(g)]
        gate = _nt_dot(kmh_ref[g], q) + _nt_dot(kml_ref[g], q)
        gate = jnp.where(valid, gate, NEG)
        chosen = jnp.zeros((n_blocks, t), jnp.bool_)
        for _ in range(MOBA_TOPK):
            top = jnp.max(gate, axis=0, keepdims=True)
            first = jnp.min(jnp.where(gate == top, blk, n_blocks), axis=0, keepdims=True)
            pick = blk == first
            chosen = jnp.logical_or(chosen, pick)
            gate = jnp.where(pick, -jnp.inf, gate)
        selb_ref[g] = jnp.where(jnp.logical_and(chosen, valid), 0.0, NEG)

        x = _nt_dot(k_ref[pl.ds(own0, t), _head_cols(g)], q) * SCORE_LOG2 + kbias_ref[g]
        x = jnp.where(col >= row, x, NEG)
        m = jnp.max(x, axis=0, keepdims=True)
        p = jnp.exp2(x - m)
        l = jnp.sum(p, axis=0, keepdims=True)
        acc_ref[g] = jnp.dot(vt_ref[g, :, pl.ds(own0, t)], p.astype(BF16),
                             preferred_element_type=F32)
        stats += [m, l]

    def body(j, stats):
        stats = list(stats)
        k0 = pl.multiple_of(j * t, t)
        dots = [_nt_dot(k_ref[pl.ds(k0, t), _head_cols(g)], q_ref[:, _head_cols(g)]) for g in heads]
        ps, alphas = [], []
        for g in heads:
            m, l = stats[2 * g], stats[2 * g + 1]
            x = dots[g] * SCORE_LOG2 + kbias_ref[g]
            shift = slope_log2[g] * ((j - i) * t).astype(F32) + selb_ref[g, pl.ds(j, 1), :]
            m_new = jnp.maximum(m, jnp.max(x, axis=0, keepdims=True) + shift)
            alpha = jnp.exp2(m - m_new)
            p = jnp.exp2(x - (m_new - shift))
            stats[2 * g] = m_new
            stats[2 * g + 1] = alpha * l + jnp.sum(p, axis=0, keepdims=True)
            ps.append(p.astype(BF16))
            alphas.append(alpha)
        for g in heads:
            acc_ref[g] = alphas[g] * acc_ref[g] + jnp.dot(
                vt_ref[g, :, pl.ds(k0, t)], ps[g], preferred_element_type=F32)
        return tuple(stats)

    stats = lax.fori_loop(0, i, body, tuple(stats))
    for g in heads:
        o_ref[:, _head_cols(g)] = (acc_ref[g] / stats[2 * g + 1]).T.astype(o_ref.dtype)


def moba_attention(qk, p2, slopes, batch, seq):
    t = ATT_TILE
    nq = seq // t
    n_blocks = seq // MOBA_BLOCK
    grid_spec = pltpu.PrefetchScalarGridSpec(
        num_scalar_prefetch=1,
        grid=(batch, N_GROUPS, nq),
        in_specs=[pl.BlockSpec((t, GROUP_WIDTH), lambda b, h, i, s: (b * nq + i, h)),
                  pl.BlockSpec((seq, GROUP_WIDTH), lambda b, h, i, s: (b, N_GROUPS + h)),
                  pl.BlockSpec((seq, GROUP_WIDTH), lambda b, h, i, s: (b, h))],
        out_specs=pl.BlockSpec((t, GROUP_WIDTH), lambda b, h, i, s: (b * nq + i, h)),
        scratch_shapes=[pltpu.VMEM((HEADS_PER_STEP, HEAD_DIM, seq), BF16),
                        pltpu.VMEM((HEADS_PER_STEP, n_blocks, HEAD_DIM), BF16),
                        pltpu.VMEM((HEADS_PER_STEP, n_blocks, HEAD_DIM), BF16),
                        pltpu.VMEM((HEADS_PER_STEP, t, t), F32),
                        pltpu.VMEM((HEADS_PER_STEP, n_blocks, t), F32),
                        pltpu.VMEM((HEADS_PER_STEP, HEAD_DIM, t), F32)],
    )
    return pl.pallas_call(
        functools.partial(_moba_kernel, n_blocks=n_blocks),
        grid_spec=grid_spec,
        out_shape=jax.ShapeDtypeStruct((batch * seq, WIDTH), BF16),
        compiler_params=_params(("arbitrary", "arbitrary", "arbitrary")),
        name="moba_attention",
    )(slopes, qk, qk, p2)


def _layer(x, g_mix, w_in, b_gate, g_q, g_k, w_bm, w_bs, w_out, g_mlp, w_up, w_down, slopes,
           batch, seq):
    h = rms_norm(x, g_mix)
    qk_gain = jnp.concatenate([jnp.tile(g_q, N_HEADS), jnp.tile(g_k, N_HEADS)]).reshape(1, 2 * WIDTH)
    qk = matmul(_mm_headnorm_kernel, h, w_in, [(qk_gain, "row")], n=2 * WIDTH, col_off=0,
                out_dtype=BF16, tm=1024, tn=1024, name="in_proj_qk_moba")
    p2 = matmul(_mm_plain_kernel, h, w_in, [], n=4 * WIDTH, col_off=2 * WIDTH,
                out_dtype=BF16, tm=1024, tn=1024, name="in_proj_rest")
    gates = matmul(_mm_sigmoid_kernel, h, w_in, [(b_gate.reshape(1, -1), "row")], n=2 * D_MODEL,
                   col_off=6 * WIDTH, out_dtype=F32, tm=1024, tn=1024, name="in_proj_gates")
    ya = moba_attention(qk, p2, slopes, batch, seq)
    yb = sb_attention(p2, batch, seq)
    merged = merge_branches(ya, yb, w_bm, w_bs, gates)
    x = matmul(_mm_residual_kernel, merged, w_out, [(x, "tile")], n=D_MODEL, col_off=0,
               out_dtype=F32, tm=1024, tn=512, name="out_proj")
    h2 = rms_norm(x, g_mlp)
    u = matmul(_mm_relu2_kernel, h2, w_up, [], n=D_FF, col_off=0,
               out_dtype=BF16, tm=1024, tn=1024, name="mlp_up")
    return matmul_kgrid_residual(u, w_down, x, tm=1024, tn=1024, tk=2048, name="mlp_down")


def kernel(x, norm_mix, w_in, b_gate, q_norm, k_norm, w_branch_moba, w_branch_sb, w_out,
           norm_mlp, w_up, w_down):
    batch, seq, d = x.shape
    depth = w_in.shape[0]
    slopes = jnp.exp2(-8.0 * jnp.arange(1, N_HEADS + 1, dtype=F32) / N_HEADS)
    y = x.reshape(batch * seq, d)
    for l in range(depth):
        y = _layer(y, norm_mix[l], w_in[l].astype(BF16), b_gate[l], q_norm[l], k_norm[l],
                   w_branch_moba[l].astype(BF16), w_branch_sb[l].astype(BF16),
                   w_out[l].astype(BF16), norm_mlp[l], w_up[l].astype(BF16),
                   w_down[l].astype(BF16), slopes, batch, seq)
    return y.reshape(batch, seq, d)
```

```python
import functools

import jax
import jax.numpy as jnp
from jax import lax
from jax.experimental import pallas as pl
from jax.experimental.pallas import tpu as pltpu

D_MODEL = 4096
HEAD_DIM = 128
N_HEADS = 16
WIDTH = N_HEADS * HEAD_DIM
MOBA_BLOCK = 256
MOBA_TOPK = 3
D_FF = 4 * D_MODEL
RMS_EPS = 1e-6
NEG = -1e30
SCALE = HEAD_DIM ** -0.5

VMEM_LIMIT_BYTES = 56 * 1024 * 1024

F32 = jnp.float32
BF16 = jnp.bfloat16


def _params(semantics):
    return pltpu.CompilerParams(dimension_semantics=semantics, vmem_limit_bytes=VMEM_LIMIT_BYTES)


def _rms_kernel(x_ref, g_ref, o_ref):
    x = x_ref[...]
    ms = jnp.mean(x * x, axis=-1, keepdims=True)
    o_ref[...] = ((x * lax.rsqrt(ms + RMS_EPS)) * g_ref[...]).astype(o_ref.dtype)


def rms_norm(x, g, *, tm=256):
    m, d = x.shape
    return pl.pallas_call(
        _rms_kernel,
        grid=(m // tm,),
        in_specs=[pl.BlockSpec((tm, d), lambda i: (i, 0)),
                  pl.BlockSpec((1, d), lambda i: (0, 0))],
        out_specs=pl.BlockSpec((tm, d), lambda i: (i, 0)),
        out_shape=jax.ShapeDtypeStruct((m, d), BF16),
        compiler_params=_params(("arbitrary",)),
        name="rms_norm",
    )(x, g.reshape(1, d))


def _cast_kernel(w_ref, o_ref):
    o_ref[...] = w_ref[...].astype(o_ref.dtype)


def layer_weight_bf16(w, layer, *, tr=512, tc=4096):
    _, rows, cols = w.shape
    tc = min(tc, cols)
    return pl.pallas_call(
        _cast_kernel,
        grid=(rows // tr, cols // tc),
        in_specs=[pl.BlockSpec((None, tr, tc), lambda i, j: (layer, i, j))],
        out_specs=pl.BlockSpec((tr, tc), lambda i, j: (i, j)),
        out_shape=jax.ShapeDtypeStruct((rows, cols), BF16),
        compiler_params=_params(("arbitrary", "arbitrary")),
        name="weight_cast",
    )(w)


def _mm_plain_kernel(a_ref, w_ref, o_ref):
    o_ref[...] = jnp.dot(a_ref[...], w_ref[...], preferred_element_type=F32).astype(o_ref.dtype)


def _mm_headnorm_kernel(a_ref, w_ref, g_ref, o_ref):
    acc = jnp.dot(a_ref[...], w_ref[...], preferred_element_type=F32)
    for c in range(acc.shape[1] // HEAD_DIM):
        sl = slice(c * HEAD_DIM, (c + 1) * HEAD_DIM)
        y = acc[:, sl]
        ms = jnp.mean(y * y, axis=-1, keepdims=True)
        o_ref[:, sl] = ((y * lax.rsqrt(ms + RMS_EPS)) * g_ref[:, sl]).astype(o_ref.dtype)


def _mm_sigmoid_kernel(a_ref, w_ref, b_ref, o_ref):
    acc = jnp.dot(a_ref[...], w_ref[...], preferred_element_type=F32)
    o_ref[...] = jax.nn.sigmoid(acc + b_ref[...]).astype(o_ref.dtype)


def _mm_relu2_kernel(a_ref, w_ref, o_ref):
    acc = jnp.dot(a_ref[...], w_ref[...], preferred_element_type=F32)
    o_ref[...] = jnp.square(jnp.maximum(acc, 0.0)).astype(o_ref.dtype)


def _mm_residual_kernel(a_ref, w_ref, r_ref, o_ref):
    acc = jnp.dot(a_ref[...], w_ref[...], preferred_element_type=F32)
    o_ref[...] = r_ref[...] + acc


def matmul(kernel, a, w, extras, *, n, col_off, out_dtype, tm, tn, name):
    m, k = a.shape
    off = col_off // tn
    assert off * tn == col_off and n % tn == 0 and m % tm == 0
    in_specs = [pl.BlockSpec((tm, k), lambda i, j: (i, 0)),
                pl.BlockSpec((k, tn), lambda i, j: (0, j + off))]
    args = [a, w]
    for arr, kind in extras:
        if kind == "row":
            in_specs.append(pl.BlockSpec((1, tn), lambda i, j: (0, j)))
        else:
            in_specs.append(pl.BlockSpec((tm, tn), lambda i, j: (i, j)))
        args.append(arr)
    return pl.pallas_call(
        kernel,
        grid=(m // tm, n // tn),
        in_specs=in_specs,
        out_specs=pl.BlockSpec((tm, tn), lambda i, j: (i, j)),
        out_shape=jax.ShapeDtypeStruct((m, n), out_dtype),
        compiler_params=_params(("arbitrary", "arbitrary")),
        name=name,
    )(*args)


def _mm_kgrid_residual_kernel(a_ref, w_ref, r_ref, o_ref, acc_ref):
    kk = pl.program_id(2)

    @pl.when(kk == 0)
    def _():
        acc_ref[...] = jnp.zeros_like(acc_ref)

    acc_ref[...] += jnp.dot(a_ref[...], w_ref[...], preferred_element_type=F32)

    @pl.when(kk == pl.num_programs(2) - 1)
    def _():
        o_ref[...] = r_ref[...] + acc_ref[...]


def matmul_kgrid_residual(a, w, r, *, tm, tn, tk, name):
    m, k = a.shape
    n = w.shape[1]
    return pl.pallas_call(
        _mm_kgrid_residual_kernel,
        grid=(m // tm, n // tn, k // tk),
        in_specs=[pl.BlockSpec((tm, tk), lambda i, j, kk: (i, kk)),
                  pl.BlockSpec((tk, tn), lambda i, j, kk: (kk, j)),
                  pl.BlockSpec((tm, tn), lambda i, j, kk: (i, j))],
        out_specs=pl.BlockSpec((tm, tn), lambda i, j, kk: (i, j)),
        out_shape=jax.ShapeDtypeStruct((m, n), F32),
        scratch_shapes=[pltpu.VMEM((tm, tn), F32)],
        compiler_params=_params(("arbitrary", "arbitrary", "arbitrary")),
        name=name,
    )(a, w, r)


def _merge_kernel(ya_ref, yb_ref, wa_ref, wb_ref, ga_ref, gb_ref, o_ref):
    pa = jnp.dot(ya_ref[...], wa_ref[...], preferred_element_type=F32)
    pb = jnp.dot(yb_ref[...], wb_ref[...], preferred_element_type=F32)
    o_ref[...] = (ga_ref[...] * pa + gb_ref[...] * pb).astype(o_ref.dtype)


def merge_branches(ya, yb, wa, wb, gates, *, tm=1024, tn=512):
    m, k = ya.shape
    n = wa.shape[1]
    goff = n // tn
    return pl.pallas_call(
        _merge_kernel,
        grid=(m // tm, n // tn),
        in_specs=[pl.BlockSpec((tm, k), lambda i, j: (i, 0)),
                  pl.BlockSpec((tm, k), lambda i, j: (i, 0)),
                  pl.BlockSpec((k, tn), lambda i, j: (0, j)),
                  pl.BlockSpec((k, tn), lambda i, j: (0, j)),
                  pl.BlockSpec((tm, tn), lambda i, j: (i, j)),
                  pl.BlockSpec((tm, tn), lambda i, j: (i, j + goff))],
        out_specs=pl.BlockSpec((tm, tn), lambda i, j: (i, j)),
        out_shape=jax.ShapeDtypeStruct((m, n), BF16),
        compiler_params=_params(("arbitrary", "arbitrary")),
        name="merge_branches",
    )(ya, yb, wa, wb, gates, gates)


ATT_TILE = 256
HEADS_PER_STEP = 4
GROUP_WIDTH = HEADS_PER_STEP * HEAD_DIM
N_GROUPS = N_HEADS // HEADS_PER_STEP

LOG2E = 1.4426950408889634
INV_LN2 = LOG2E
SCORE_LOG2 = SCALE * LOG2E
SB_EXIT_LOG2 = -160.0
VT_ROWS = HEAD_DIM + 16


def _head_cols(g):
    return slice(g * HEAD_DIM, (g + 1) * HEAD_DIM)


def _transpose_values(v_ref, vt_ref, g):
    n_chunks = v_ref.shape[0] // ATT_TILE

    def body(c, carry):
        r0 = pl.multiple_of(c * ATT_TILE, ATT_TILE)
        vt_ref[g, :HEAD_DIM, pl.ds(r0, ATT_TILE)] = (
            v_ref[pl.ds(r0, ATT_TILE), _head_cols(g)].astype(F32).T.astype(BF16))
        return carry

    lax.fori_loop(0, n_chunks, body, 0)


def _nt_dot(a, b):
    return lax.dot_general(a, b, (((1,), (1,)), ((), ())), preferred_element_type=F32)


def _neg_abs(x):
    bits = lax.bitcast_convert_type(x, jnp.uint32) | jnp.uint32(0x80000000)
    return lax.bitcast_convert_type(bits, F32)


def _sb_kernel(q_ref, k_ref, v_ref, o_ref, vt_ref, acc_ref):
    i = pl.program_id(2)
    t = ATT_TILE
    heads = range(HEADS_PER_STEP)

    @pl.when(i == 0)
    def _():
        for g in heads:
            _transpose_values(v_ref, vt_ref, g)

    row = lax.broadcasted_iota(jnp.int32, (t, t), 0)
    col = lax.broadcasted_iota(jnp.int32, (t, t), 1)
    upper = (col > row).astype(BF16)

    def block(j, carries, diagonal):
        k0 = pl.multiple_of(j * t, t)
        zs = [_nt_dot(k_ref[pl.ds(k0, t), _head_cols(g)], q_ref[:, _head_cols(g)]) for g in heads]
        log_betas, log_1ms, cums = [], [], []
        for g in heads:
            z = zs[g] * SCORE_LOG2
            log_beta = jnp.minimum(z, 0.0) - jnp.log(1.0 + jnp.exp2(_neg_abs(z))) * INV_LN2
            log_1m = log_beta - z
            if diagonal:
                log_1m = jnp.where(row < col, log_1m, 0.0)
            hi = log_1m.astype(BF16)
            lo = (log_1m - hi.astype(F32)).astype(BF16)
            cums.append(jnp.dot(upper, hi, preferred_element_type=F32)
                        + jnp.dot(upper, lo, preferred_element_type=F32))
            log_betas.append(log_beta)
            log_1ms.append(log_1m)
        pvs, new_carries = [], []
        for g in heads:
            w = jnp.exp2(log_betas[g] + (cums[g] + carries[g]))
            if diagonal:
                w = jnp.where(row < col, w, 0.0)
            pvs.append(jnp.dot(vt_ref[g, :, pl.ds(k0, t)], w.astype(BF16),
                               preferred_element_type=F32))
            new_carries.append(carries[g] + jnp.sum(log_1ms[g], axis=0, keepdims=True))
        return new_carries, pvs

    def any_alive(carries):
        top = carries[0]
        for c in carries[1:]:
            top = jnp.maximum(top, c)
        return jnp.max(top, axis=1, keepdims=True)[0, 0] > SB_EXIT_LOG2

    carries, pvs = block(i, [jnp.zeros((1, t), F32) for _ in heads], True)
    for g in heads:
        acc_ref[g] = pvs[g]

    def cond(state):
        step, alive = state[0], state[1]
        return jnp.logical_and(step < i, alive)

    def body(state):
        step = state[0]
        carries, pvs = block(i - 1 - step, list(state[2:]), False)
        for g in heads:
            acc_ref[g] += pvs[g]
        return (step + 1, any_alive(carries), *carries)

    lax.while_loop(cond, body, (jnp.int32(0), any_alive(carries), *carries))
    for g in heads:
        o_ref[:, _head_cols(g)] = acc_ref[g].T.astype(o_ref.dtype)


def sb_attention(p2, batch, seq):
    t = ATT_TILE
    nq = seq // t
    qc, kc, vc = N_GROUPS, 2 * N_GROUPS, 3 * N_GROUPS
    return pl.pallas_call(
        _sb_kernel,
        grid=(batch, N_GROUPS, nq),
        in_specs=[pl.BlockSpec((t, GROUP_WIDTH), lambda b, h, i: (b * nq + i, qc + h)),
                  pl.BlockSpec((seq, GROUP_WIDTH), lambda b, h, i: (b, kc + h)),
                  pl.BlockSpec((seq, GROUP_WIDTH), lambda b, h, i: (b, vc + h))],
        out_specs=pl.BlockSpec((t, GROUP_WIDTH), lambda b, h, i: (b * nq + i, h)),
        out_shape=jax.ShapeDtypeStruct((batch * seq, WIDTH), BF16),
        scratch_shapes=[pltpu.VMEM((HEADS_PER_STEP, HEAD_DIM, seq), BF16),
                        pltpu.VMEM((HEADS_PER_STEP, HEAD_DIM, t), F32)],
        compiler_params=_params(("arbitrary", "arbitrary", "arbitrary")),
        name="sb_attention",
    )(p2, p2, p2)


def _moba_kernel(slopes_ref, q_ref, k_ref, v_ref, o_ref, vt_ref, kmh_ref, kml_ref, kbias_ref,
                 selb_ref, acc_ref, *p_refs, n_blocks):
    hg = pl.program_id(1)
    i = pl.program_id(2)
    t = ATT_TILE
    heads = range(HEADS_PER_STEP)
    n_heads = HEADS_PER_STEP
    seq = k_ref.shape[0]

    row = lax.broadcasted_iota(jnp.int32, (t, t), 0)
    col = lax.broadcasted_iota(jnp.int32, (t, t), 1)
    slope_log2 = [slopes_ref[hg * HEADS_PER_STEP + g] * LOG2E for g in heads]

    @pl.when(i == 0)
    def _():
        for g in heads:
            _transpose_values(v_ref, vt_ref, g)
            vt_ref[g, HEAD_DIM:, :] = jnp.ones((VT_ROWS - HEAD_DIM, seq + t), BF16)
            vt_ref[g, :HEAD_DIM, pl.ds(seq, t)] = jnp.zeros((HEAD_DIM, t), BF16)
            km = jnp.mean(k_ref[:, _head_cols(g)].astype(F32).reshape(n_blocks, t, HEAD_DIM), axis=1)
            hi = km.astype(BF16)
            kmh_ref[g] = hi
            kml_ref[g] = (km - hi.astype(F32)).astype(BF16)
            kbias_ref[g] = row.astype(F32) * slope_log2[g]

    blk = lax.broadcasted_iota(jnp.int32, (n_blocks, t), 0)
    valid = blk < i

    def scores(j, g):
        k0 = pl.multiple_of(j * t, t)
        return _nt_dot(k_ref[pl.ds(k0, t), _head_cols(g)], q_ref[:, _head_cols(g)])

    def weighted_sum(j, g, slot, alpha):
        k0 = pl.multiple_of(j * t, t)
        k1 = pl.multiple_of((j + 1) * t, t)
        acc_ref[g] = (alpha * acc_ref[g]
                      + jnp.dot(vt_ref[g, :, pl.ds(k0, t)], p_refs[g][slot, 0],
                                preferred_element_type=F32)
                      + jnp.dot(vt_ref[g, :, pl.ds(k1, t)], p_refs[g][slot, 1],
                                preferred_element_type=F32))

    ms = []
    for g in heads:
        q = q_ref[:, _head_cols(g)]
        gate = _nt_dot(kmh_ref[g], q) + _nt_dot(kml_ref[g], q)
        gate = jnp.where(valid, gate, NEG)
        chosen = jnp.zeros((n_blocks, t), jnp.bool_)
        for _ in range(MOBA_TOPK):
            top = jnp.max(gate, axis=0, keepdims=True)
            first = jnp.min(jnp.where(gate == top, blk, n_blocks), axis=0, keepdims=True)
            pick = blk == first
            chosen = jnp.logical_or(chosen, pick)
            gate = jnp.where(pick, -jnp.inf, gate)
        selb_ref[g] = jnp.where(jnp.logical_and(chosen, valid), 0.0, NEG)

        x = jnp.where(col >= row, scores(i, g) + kbias_ref[g], NEG)
        m = jnp.max(x, axis=0, keepdims=True)
        ms.append(m)
        p_refs[g][0, 0] = jnp.exp2(x - m).astype(BF16)
        p_refs[g][0, 1] = jnp.zeros((t, t), BF16)
        acc_ref[g] = jnp.zeros((VT_ROWS, t), F32)

    def body(step, state):
        prev, state = state[0], state[1:]
        alphas, ms = state[:n_heads], state[n_heads:]
        slot = step & 1
        j = 2 * step
        for g in heads:
            weighted_sum(prev, g, slot, alphas[g])
        dots = [(scores(j, g), scores(j + 1, g)) for g in heads]
        new_alphas, new_ms = [], []
        for g in heads:
            x0 = dots[g][0] + kbias_ref[g]
            x1 = dots[g][1] + kbias_ref[g]
            base = slope_log2[g] * ((j - i) * t).astype(F32)
            shift0 = base + selb_ref[g, pl.ds(j, 1), :]
            shift1 = (base + slope_log2[g] * t) + selb_ref[g, pl.ds(j + 1, 1), :]
            m_new = jnp.maximum(ms[g], jnp.maximum(jnp.max(x0, axis=0, keepdims=True) + shift0,
                                                   jnp.max(x1, axis=0, keepdims=True) + shift1))
            new_alphas.append(jnp.exp2(ms[g] - m_new))
            p_refs[g][1 - slot, 0] = jnp.exp2(x0 - (m_new - shift0)).astype(BF16)
            p_refs[g][1 - slot, 1] = jnp.exp2(x1 - (m_new - shift1)).astype(BF16)
            new_ms.append(m_new)
        return (j, *new_alphas, *new_ms)

    n_steps = (i + 1) >> 1
    ones = [jnp.ones((1, t), F32) for _ in heads]
    state = lax.fori_loop(0, n_steps, body, (i, *ones, *ms))
    prev, alphas = state[0], state[1:1 + n_heads]
    for g in heads:
        weighted_sum(prev, g, n_steps & 1, alphas[g])
        acc = acc_ref[g]
        o_ref[:, _head_cols(g)] = (acc[:HEAD_DIM] / acc[HEAD_DIM:HEAD_DIM + 1]).T.astype(o_ref.dtype)


def moba_attention(qk, p2, slopes, batch, seq):
    t = ATT_TILE
    nq = seq // t
    n_blocks = seq // MOBA_BLOCK
    grid_spec = pltpu.PrefetchScalarGridSpec(
        num_scalar_prefetch=1,
        grid=(batch, N_GROUPS, nq),
        in_specs=[pl.BlockSpec((t, GROUP_WIDTH), lambda b, h, i, s: (b * nq + i, h)),
                  pl.BlockSpec((seq, GROUP_WIDTH), lambda b, h, i, s: (b, N_GROUPS + h)),
                  pl.BlockSpec((seq, GROUP_WIDTH), lambda b, h, i, s: (b, h))],
        out_specs=pl.BlockSpec((t, GROUP_WIDTH), lambda b, h, i, s: (b * nq + i, h)),
        scratch_shapes=[pltpu.VMEM((HEADS_PER_STEP, VT_ROWS, seq + t), BF16),
                        pltpu.VMEM((HEADS_PER_STEP, n_blocks, HEAD_DIM), BF16),
                        pltpu.VMEM((HEADS_PER_STEP, n_blocks, HEAD_DIM), BF16),
                        pltpu.VMEM((HEADS_PER_STEP, t, t), F32),
                        pltpu.VMEM((HEADS_PER_STEP, n_blocks, t), F32),
                        pltpu.VMEM((HEADS_PER_STEP, VT_ROWS, t), F32)]
                       + [pltpu.VMEM((2, 2, t, t), BF16) for _ in range(HEADS_PER_STEP)],
    )
    return pl.pallas_call(
        functools.partial(_moba_kernel, n_blocks=n_blocks),
        grid_spec=grid_spec,
        out_shape=jax.ShapeDtypeStruct((batch * seq, WIDTH), BF16),
        compiler_params=_params(("arbitrary", "arbitrary", "arbitrary")),
        name="moba_attention",
    )(slopes, qk, qk, p2)


def _layer(x, g_mix, w_in, b_gate, g_q, g_k, w_bm, w_bs, w_out, g_mlp, w_up, w_down, slopes,
           batch, seq):
    h = rms_norm(x, g_mix)
    qk_gain = jnp.concatenate([jnp.tile(g_q * SCORE_LOG2, N_HEADS),
                               jnp.tile(g_k, N_HEADS)]).reshape(1, 2 * WIDTH)
    qk = matmul(_mm_headnorm_kernel, h, w_in, [(qk_gain, "row")], n=2 * WIDTH, col_off=0,
                out_dtype=BF16, tm=1024, tn=1024, name="in_proj_qk_moba")
    p2 = matmul(_mm_plain_kernel, h, w_in, [], n=4 * WIDTH, col_off=2 * WIDTH,
                out_dtype=BF16, tm=1024, tn=1024, name="in_proj_rest")
    gates = matmul(_mm_sigmoid_kernel, h, w_in, [(b_gate.reshape(1, -1), "row")], n=2 * D_MODEL,
                   col_off=6 * WIDTH, out_dtype=F32, tm=1024, tn=1024, name="in_proj_gates")
    ya = moba_attention(qk, p2, slopes, batch, seq)
    yb = sb_attention(p2, batch, seq)
    merged = merge_branches(ya, yb, w_bm, w_bs, gates)
    x = matmul(_mm_residual_kernel, merged, w_out, [(x, "tile")], n=D_MODEL, col_off=0,
               out_dtype=F32, tm=1024, tn=512, name="out_proj")
    h2 = rms_norm(x, g_mlp)
    u = matmul(_mm_relu2_kernel, h2, w_up, [], n=D_FF, col_off=0,
               out_dtype=BF16, tm=1024, tn=1024, name="mlp_up")
    return matmul_kgrid_residual(u, w_down, x, tm=1024, tn=1024, tk=2048, name="mlp_down")


def kernel(x, norm_mix, w_in, b_gate, q_norm, k_norm, w_branch_moba, w_branch_sb, w_out,
           norm_mlp, w_up, w_down):
    batch, seq, d = x.shape
    depth = w_in.shape[0]
    slopes = jnp.exp2(-8.0 * jnp.arange(1, N_HEADS + 1, dtype=F32) / N_HEADS)
    y = x.reshape(batch * seq, d)
    for l in range(depth):
        y = _layer(y, norm_mix[l], layer_weight_bf16(w_in, l), b_gate[l], q_norm[l], k_norm[l],
                   layer_weight_bf16(w_branch_moba, l), layer_weight_bf16(w_branch_sb, l),
                   layer_weight_bf16(w_out, l), norm_mlp[l], layer_weight_bf16(w_up, l),
                   layer_weight_bf16(w_down, l), slopes, batch, seq)
    return y.reshape(batch, seq, d)
```

```python
import functools

import jax
import jax.numpy as jnp
from jax import lax
from jax.experimental import pallas as pl
from jax.experimental.pallas import tpu as pltpu

D_MODEL = 4096
HEAD_DIM = 128
N_HEADS = 16
WIDTH = N_HEADS * HEAD_DIM
MOBA_BLOCK = 256
MOBA_TOPK = 3
D_FF = 4 * D_MODEL
RMS_EPS = 1e-6
NEG = -1e30
SCALE = HEAD_DIM ** -0.5

VMEM_LIMIT_BYTES = 56 * 1024 * 1024

F32 = jnp.float32
BF16 = jnp.bfloat16


def _params(semantics):
    return pltpu.CompilerParams(dimension_semantics=semantics, vmem_limit_bytes=VMEM_LIMIT_BYTES)


def _rms_kernel(x_ref, g_ref, o_ref):
    x = x_ref[...]
    ms = jnp.mean(x * x, axis=-1, keepdims=True)
    o_ref[...] = ((x * lax.rsqrt(ms + RMS_EPS)) * g_ref[...]).astype(o_ref.dtype)


def rms_norm(x, g, *, tm=256):
    m, d = x.shape
    return pl.pallas_call(
        _rms_kernel,
        grid=(m // tm,),
        in_specs=[pl.BlockSpec((tm, d), lambda i: (i, 0)),
                  pl.BlockSpec((1, d), lambda i: (0, 0))],
        out_specs=pl.BlockSpec((tm, d), lambda i: (i, 0)),
        out_shape=jax.ShapeDtypeStruct((m, d), BF16),
        compiler_params=_params(("arbitrary",)),
        name="rms_norm",
    )(x, g.reshape(1, d))


def _cast_kernel(w_ref, o_ref):
    o_ref[...] = w_ref[...].astype(o_ref.dtype)


def layer_weight_bf16(w, layer, *, tr=512, tc=4096):
    _, rows, cols = w.shape
    tc = min(tc, cols)
    return pl.pallas_call(
        _cast_kernel,
        grid=(rows // tr, cols // tc),
        in_specs=[pl.BlockSpec((None, tr, tc), lambda i, j: (layer, i, j))],
        out_specs=pl.BlockSpec((tr, tc), lambda i, j: (i, j)),
        out_shape=jax.ShapeDtypeStruct((rows, cols), BF16),
        compiler_params=_params(("arbitrary", "arbitrary")),
        name="weight_cast",
    )(w)


def _mm_plain_kernel(a_ref, w_ref, o_ref):
    o_ref[...] = jnp.dot(a_ref[...], w_ref[...], preferred_element_type=F32).astype(o_ref.dtype)


def _mm_headnorm_kernel(a_ref, w_ref, g_ref, o_ref):
    acc = jnp.dot(a_ref[...], w_ref[...], preferred_element_type=F32)
    for c in range(acc.shape[1] // HEAD_DIM):
        sl = slice(c * HEAD_DIM, (c + 1) * HEAD_DIM)
        y = acc[:, sl]
        ms = jnp.mean(y * y, axis=-1, keepdims=True)
        o_ref[:, sl] = ((y * lax.rsqrt(ms + RMS_EPS)) * g_ref[:, sl]).astype(o_ref.dtype)


def _mm_sigmoid_kernel(a_ref, w_ref, b_ref, o_ref):
    acc = jnp.dot(a_ref[...], w_ref[...], preferred_element_type=F32)
    o_ref[...] = jax.nn.sigmoid(acc + b_ref[...]).astype(o_ref.dtype)


def _mm_relu2_kernel(a_ref, w_ref, o_ref):
    acc = jnp.dot(a_ref[...], w_ref[...], preferred_element_type=F32)
    o_ref[...] = jnp.square(jnp.maximum(acc, 0.0)).astype(o_ref.dtype)


def _mm_residual_kernel(a_ref, w_ref, r_ref, o_ref):
    acc = jnp.dot(a_ref[...], w_ref[...], preferred_element_type=F32)
    o_ref[...] = r_ref[...] + acc


def matmul(kernel, a, w, extras, *, n, col_off, out_dtype, tm, tn, name):
    m, k = a.shape
    off = col_off // tn
    assert off * tn == col_off and n % tn == 0 and m % tm == 0
    in_specs = [pl.BlockSpec((tm, k), lambda i, j: (i, 0)),
                pl.BlockSpec((k, tn), lambda i, j: (0, j + off))]
    args = [a, w]
    for arr, kind in extras:
        if kind == "row":
            in_specs.append(pl.BlockSpec((1, tn), lambda i, j: (0, j)))
        else:
            in_specs.append(pl.BlockSpec((tm, tn), lambda i, j: (i, j)))
        args.append(arr)
    return pl.pallas_call(
        kernel,
        grid=(m // tm, n // tn),
        in_specs=in_specs,
        out_specs=pl.BlockSpec((tm, tn), lambda i, j: (i, j)),
        out_shape=jax.ShapeDtypeStruct((m, n), out_dtype),
        compiler_params=_params(("arbitrary", "arbitrary")),
        name=name,
    )(*args)


def _mm_kgrid_residual_kernel(a_ref, w_ref, r_ref, o_ref, acc_ref):
    kk = pl.program_id(2)

    @pl.when(kk == 0)
    def _():
        acc_ref[...] = jnp.zeros_like(acc_ref)

    acc_ref[...] += jnp.dot(a_ref[...], w_ref[...], preferred_element_type=F32)

    @pl.when(kk == pl.num_programs(2) - 1)
    def _():
        o_ref[...] = r_ref[...] + acc_ref[...]


def matmul_kgrid_residual(a, w, r, *, tm, tn, tk, name):
    m, k = a.shape
    n = w.shape[1]
    return pl.pallas_call(
        _mm_kgrid_residual_kernel,
        grid=(m // tm, n // tn, k // tk),
        in_specs=[pl.BlockSpec((tm, tk), lambda i, j, kk: (i, kk)),
                  pl.BlockSpec((tk, tn), lambda i, j, kk: (kk, j)),
                  pl.BlockSpec((tm, tn), lambda i, j, kk: (i, j))],
        out_specs=pl.BlockSpec((tm, tn), lambda i, j, kk: (i, j)),
        out_shape=jax.ShapeDtypeStruct((m, n), F32),
        scratch_shapes=[pltpu.VMEM((tm, tn), F32)],
        compiler_params=_params(("arbitrary", "arbitrary", "arbitrary")),
        name=name,
    )(a, w, r)


def _merge_kernel(ya_ref, yb_ref, wa_ref, wb_ref, ga_ref, gb_ref, o_ref):
    pa = jnp.dot(ya_ref[...], wa_ref[...], preferred_element_type=F32)
    pb = jnp.dot(yb_ref[...], wb_ref[...], preferred_element_type=F32)
    o_ref[...] = (ga_ref[...] * pa + gb_ref[...] * pb).astype(o_ref.dtype)


def merge_branches(ya, yb, wa, wb, gates, *, tm=1024, tn=512):
    m, k = ya.shape
    n = wa.shape[1]
    goff = n // tn
    return pl.pallas_call(
        _merge_kernel,
        grid=(m // tm, n // tn),
        in_specs=[pl.BlockSpec((tm, k), lambda i, j: (i, 0)),
                  pl.BlockSpec((tm, k), lambda i, j: (i, 0)),
                  pl.BlockSpec((k, tn), lambda i, j: (0, j)),
                  pl.BlockSpec((k, tn), lambda i, j: (0, j)),
                  pl.BlockSpec((tm, tn), lambda i, j: (i, j)),
                  pl.BlockSpec((tm, tn), lambda i, j: (i, j + goff))],
        out_specs=pl.BlockSpec((tm, tn), lambda i, j: (i, j)),
        out_shape=jax.ShapeDtypeStruct((m, n), BF16),
        compiler_params=_params(("arbitrary", "arbitrary")),
        name="merge_branches",
    )(ya, yb, wa, wb, gates, gates)


ATT_TILE = 256
HEADS_PER_STEP = 4
GROUP_WIDTH = HEADS_PER_STEP * HEAD_DIM
N_GROUPS = N_HEADS // HEADS_PER_STEP

LOG2E = 1.4426950408889634
INV_LN2 = LOG2E
SCORE_LOG2 = SCALE * LOG2E
ZERO_WEIGHT_LOG2 = -160.0
VT_ROWS = HEAD_DIM + 16


def _head_cols(g):
    return slice(g * HEAD_DIM, (g + 1) * HEAD_DIM)


def _transpose_values(v_ref, vt_ref, g):
    n_chunks = v_ref.shape[0] // ATT_TILE

    def body(c, carry):
        r0 = pl.multiple_of(c * ATT_TILE, ATT_TILE)
        vt_ref[g, :HEAD_DIM, pl.ds(r0, ATT_TILE)] = (
            v_ref[pl.ds(r0, ATT_TILE), _head_cols(g)].astype(F32).T.astype(BF16))
        return carry

    lax.fori_loop(0, n_chunks, body, 0)


def _nt_dot(a, b):
    return lax.dot_general(a, b, (((1,), (1,)), ((), ())), preferred_element_type=F32)


def _neg_abs(x):
    bits = lax.bitcast_convert_type(x, jnp.uint32) | jnp.uint32(0x80000000)
    return lax.bitcast_convert_type(bits, F32)


def _sb_kernel(q_ref, k_ref, v_ref, o_ref, vt_ref, acc_ref):
    i = pl.program_id(2)
    t = ATT_TILE
    heads = range(HEADS_PER_STEP)

    @pl.when(i == 0)
    def _():
        for g in heads:
            _transpose_values(v_ref, vt_ref, g)

    row = lax.broadcasted_iota(jnp.int32, (t, t), 0)
    col = lax.broadcasted_iota(jnp.int32, (t, t), 1)
    upper = (col > row).astype(BF16)

    def block(j, carries, diagonal):
        k0 = pl.multiple_of(j * t, t)
        zs = [_nt_dot(k_ref[pl.ds(k0, t), _head_cols(g)], q_ref[:, _head_cols(g)]) for g in heads]
        log_betas, log_1ms, cums = [], [], []
        for g in heads:
            z = zs[g] * SCORE_LOG2
            log_beta = jnp.minimum(z, 0.0) - jnp.log(1.0 + jnp.exp2(_neg_abs(z))) * INV_LN2
            log_1m = log_beta - z
            if diagonal:
                log_1m = jnp.where(row < col, log_1m, 0.0)
            hi = log_1m.astype(BF16)
            lo = (log_1m - hi.astype(F32)).astype(BF16)
            cums.append(jnp.dot(upper, hi, preferred_element_type=F32)
                        + jnp.dot(upper, lo, preferred_element_type=F32))
            log_betas.append(log_beta)
            log_1ms.append(log_1m)
        pvs, new_carries = [], []
        for g in heads:
            w = jnp.exp2(log_betas[g] + (cums[g] + carries[g]))
            if diagonal:
                w = jnp.where(row < col, w, 0.0)
            pvs.append(jnp.dot(vt_ref[g, :, pl.ds(k0, t)], w.astype(BF16),
                               preferred_element_type=F32))
            new_carries.append(carries[g] + jnp.sum(log_1ms[g], axis=0, keepdims=True))
        return new_carries, pvs

    def any_alive(carries):
        top = carries[0]
        for c in carries[1:]:
            top = jnp.maximum(top, c)
        return jnp.max(top, axis=1, keepdims=True)[0, 0] > ZERO_WEIGHT_LOG2

    carries, pvs = block(i, [jnp.zeros((1, t), F32) for _ in heads], True)
    for g in heads:
        acc_ref[g] = pvs[g]

    def cond(state):
        step, alive = state[0], state[1]
        return jnp.logical_and(step < i, alive)

    def body(state):
        step = state[0]
        carries, pvs = block(i - 1 - step, list(state[2:]), False)
        for g in heads:
            acc_ref[g] += pvs[g]
        return (step + 1, any_alive(carries), *carries)

    lax.while_loop(cond, body, (jnp.int32(0), any_alive(carries), *carries))
    for g in heads:
        o_ref[:, _head_cols(g)] = acc_ref[g].T.astype(o_ref.dtype)


def sb_attention(p2, batch, seq):
    t = ATT_TILE
    nq = seq // t
    qc, kc, vc = N_GROUPS, 2 * N_GROUPS, 3 * N_GROUPS
    return pl.pallas_call(
        _sb_kernel,
        grid=(batch, N_GROUPS, nq),
        in_specs=[pl.BlockSpec((t, GROUP_WIDTH), lambda b, h, i: (b * nq + i, qc + h)),
                  pl.BlockSpec((seq, GROUP_WIDTH), lambda b, h, i: (b, kc + h)),
                  pl.BlockSpec((seq, GROUP_WIDTH), lambda b, h, i: (b, vc + h))],
        out_specs=pl.BlockSpec((t, GROUP_WIDTH), lambda b, h, i: (b * nq + i, h)),
        out_shape=jax.ShapeDtypeStruct((batch * seq, WIDTH), BF16),
        scratch_shapes=[pltpu.VMEM((HEADS_PER_STEP, HEAD_DIM, seq), BF16),
                        pltpu.VMEM((HEADS_PER_STEP, HEAD_DIM, t), F32)],
        compiler_params=_params(("arbitrary", "arbitrary", "arbitrary")),
        name="sb_attention",
    )(p2, p2, p2)


def _moba_kernel(slopes_ref, inv_span_ref, q_ref, k_ref, v_ref, o_ref, vt_ref, kmh_ref, kml_ref,
                 kbias_ref, selb_ref, acc_ref, knorm_ref, *p_refs, n_blocks):
    hg = pl.program_id(1)
    i = pl.program_id(2)
    t = ATT_TILE
    heads = range(HEADS_PER_STEP)
    n_heads = HEADS_PER_STEP
    seq = k_ref.shape[0]

    row = lax.broadcasted_iota(jnp.int32, (t, t), 0)
    col = lax.broadcasted_iota(jnp.int32, (t, t), 1)
    slope_log2 = [slopes_ref[hg * HEADS_PER_STEP + g] * LOG2E for g in heads]
    inv_span = [inv_span_ref[hg * HEADS_PER_STEP + g] for g in heads]

    @pl.when(i == 0)
    def _():
        for g in heads:
            _transpose_values(v_ref, vt_ref, g)
            vt_ref[g, HEAD_DIM:, :] = jnp.ones((VT_ROWS - HEAD_DIM, seq + t), BF16)
            vt_ref[g, :HEAD_DIM, pl.ds(seq, t)] = jnp.zeros((HEAD_DIM, t), BF16)
            k32 = k_ref[:, _head_cols(g)].astype(F32)
            kn2 = jnp.max(jnp.sum(k32 * k32, axis=1, keepdims=True), axis=0, keepdims=True)
            knorm_ref[g] = jnp.broadcast_to(kn2, knorm_ref.shape[1:])
            km = jnp.mean(k32.reshape(n_blocks, t, HEAD_DIM), axis=1)
            hi = km.astype(BF16)
            kmh_ref[g] = hi
            kml_ref[g] = (km - hi.astype(F32)).astype(BF16)
            kbias_ref[g] = row.astype(F32) * slope_log2[g]

    blk = lax.broadcasted_iota(jnp.int32, (n_blocks, t), 0)
    valid = blk < i

    def scores(j, g):
        k0 = pl.multiple_of(j * t, t)
        return _nt_dot(k_ref[pl.ds(k0, t), _head_cols(g)], q_ref[:, _head_cols(g)])

    def weighted_sum(j, g, slot, alpha):
        k0 = pl.multiple_of(j * t, t)
        k1 = pl.multiple_of((j + 1) * t, t)
        acc_ref[g] = (alpha * acc_ref[g]
                      + jnp.dot(vt_ref[g, :, pl.ds(k0, t)], p_refs[g][slot, 0],
                                preferred_element_type=F32)
                      + jnp.dot(vt_ref[g, :, pl.ds(k1, t)], p_refs[g][slot, 1],
                                preferred_element_type=F32))

    ms, own_ps, keeps = [], [], []
    for g in heads:
        q = q_ref[:, _head_cols(g)]
        gate = _nt_dot(kmh_ref[g], q) + _nt_dot(kml_ref[g], q)
        gate = jnp.where(valid, gate, NEG)
        chosen = jnp.zeros((n_blocks, t), jnp.bool_)
        for _ in range(MOBA_TOPK):
            top = jnp.max(gate, axis=0, keepdims=True)
            first = jnp.min(jnp.where(gate == top, blk, n_blocks), axis=0, keepdims=True)
            pick = blk == first
            chosen = jnp.logical_or(chosen, pick)
            gate = jnp.where(pick, -jnp.inf, gate)
        selb_ref[g] = jnp.where(jnp.logical_and(chosen, valid), 0.0, NEG)

        x = jnp.where(col >= row, scores(i, g) + kbias_ref[g], NEG)
        m = jnp.max(x, axis=0, keepdims=True)
        ms.append(m)
        own_ps.append(jnp.exp2(x - m).astype(BF16))
        acc_ref[g] = jnp.zeros((VT_ROWS, t), F32)

        q32 = q.astype(F32)
        qn2 = jnp.max(jnp.sum(q32 * q32, axis=1, keepdims=True), axis=0, keepdims=True)
        reach = jnp.sqrt(qn2 * knorm_ref[g, :1, :1]) - jnp.min(m, axis=1, keepdims=True)
        keeps.append((reach - ZERO_WEIGHT_LOG2) * inv_span[g] + 2.0)

    keep = keeps[0]
    for other in keeps[1:]:
        keep = jnp.maximum(keep, other)
    keep = jnp.clip(keep, 0.0, float(n_blocks)).astype(jnp.int32)[0, 0]
    first_step = jnp.maximum(i - keep, 0) >> 1
    for g in heads:
        p_refs[g][first_step & 1, 0] = own_ps[g]
        p_refs[g][first_step & 1, 1] = jnp.zeros((t, t), BF16)

    def body(step, state):
        prev, state = state[0], state[1:]
        alphas, ms = state[:n_heads], state[n_heads:]
        slot = step & 1
        j = 2 * step
        for g in heads:
            weighted_sum(prev, g, slot, alphas[g])
        dots = [(scores(j, g), scores(j + 1, g)) for g in heads]
        new_alphas, new_ms = [], []
        for g in heads:
            x0 = dots[g][0] + kbias_ref[g]
            x1 = dots[g][1] + kbias_ref[g]
            base = slope_log2[g] * ((j - i) * t).astype(F32)
            shift0 = base + selb_ref[g, pl.ds(j, 1), :]
            shift1 = (base + slope_log2[g] * t) + selb_ref[g, pl.ds(j + 1, 1), :]
            m_new = jnp.maximum(ms[g], jnp.maximum(jnp.max(x0, axis=0, keepdims=True) + shift0,
                                                   jnp.max(x1, axis=0, keepdims=True) + shift1))
            new_alphas.append(jnp.exp2(ms[g] - m_new))
            p_refs[g][1 - slot, 0] = jnp.exp2(x0 - (m_new - shift0)).astype(BF16)
            p_refs[g][1 - slot, 1] = jnp.exp2(x1 - (m_new - shift1)).astype(BF16)
            new_ms.append(m_new)
        return (j, *new_alphas, *new_ms)

    n_steps = (i + 1) >> 1
    ones = [jnp.ones((1, t), F32) for _ in heads]
    state = lax.fori_loop(first_step, n_steps, body, (i, *ones, *ms))
    prev, alphas = state[0], state[1:1 + n_heads]
    for g in heads:
        weighted_sum(prev, g, n_steps & 1, alphas[g])
        acc = acc_ref[g]
        o_ref[:, _head_cols(g)] = (acc[:HEAD_DIM] / acc[HEAD_DIM:HEAD_DIM + 1]).T.astype(o_ref.dtype)


def moba_attention(qk, p2, slopes, batch, seq):
    t = ATT_TILE
    nq = seq // t
    n_blocks = seq // MOBA_BLOCK
    grid_spec = pltpu.PrefetchScalarGridSpec(
        num_scalar_prefetch=2,
        grid=(batch, N_GROUPS, nq),
        in_specs=[pl.BlockSpec((t, GROUP_WIDTH), lambda b, h, i, s, r: (b * nq + i, h)),
                  pl.BlockSpec((seq, GROUP_WIDTH), lambda b, h, i, s, r: (b, N_GROUPS + h)),
                  pl.BlockSpec((seq, GROUP_WIDTH), lambda b, h, i, s, r: (b, h))],
        out_specs=pl.BlockSpec((t, GROUP_WIDTH), lambda b, h, i, s, r: (b * nq + i, h)),
        scratch_shapes=[pltpu.VMEM((HEADS_PER_STEP, VT_ROWS, seq + t), BF16),
                        pltpu.VMEM((HEADS_PER_STEP, n_blocks, HEAD_DIM), BF16),
                        pltpu.VMEM((HEADS_PER_STEP, n_blocks, HEAD_DIM), BF16),
                        pltpu.VMEM((HEADS_PER_STEP, t, t), F32),
                        pltpu.VMEM((HEADS_PER_STEP, n_blocks, t), F32),
                        pltpu.VMEM((HEADS_PER_STEP, VT_ROWS, t), F32),
                        pltpu.VMEM((HEADS_PER_STEP, 8, HEAD_DIM), F32)]
                       + [pltpu.VMEM((2, 2, t, t), BF16) for _ in range(HEADS_PER_STEP)],
    )
    return pl.pallas_call(
        functools.partial(_moba_kernel, n_blocks=n_blocks),
        grid_spec=grid_spec,
        out_shape=jax.ShapeDtypeStruct((batch * seq, WIDTH), BF16),
        compiler_params=_params(("arbitrary", "arbitrary", "arbitrary")),
        name="moba_attention",
    )(slopes, 1.0 / (slopes * (LOG2E * MOBA_BLOCK)), qk, qk, p2)


def _layer(x, g_mix, w_in, b_gate, g_q, g_k, w_bm, w_bs, w_out, g_mlp, w_up, w_down, slopes,
           batch, seq):
    h = rms_norm(x, g_mix)
    qk_gain = jnp.concatenate([jnp.tile(g_q * SCORE_LOG2, N_HEADS),
                               jnp.tile(g_k, N_HEADS)]).reshape(1, 2 * WIDTH)
    qk = matmul(_mm_headnorm_kernel, h, w_in, [(qk_gain, "row")], n=2 * WIDTH, col_off=0,
                out_dtype=BF16, tm=1024, tn=1024, name="in_proj_qk_moba")
    p2 = matmul(_mm_plain_kernel, h, w_in, [], n=4 * WIDTH, col_off=2 * WIDTH,
                out_dtype=BF16, tm=1024, tn=1024, name="in_proj_rest")
    gates = matmul(_mm_sigmoid_kernel, h, w_in, [(b_gate.reshape(1, -1), "row")], n=2 * D_MODEL,
                   col_off=6 * WIDTH, out_dtype=F32, tm=1024, tn=1024, name="in_proj_gates")
    ya = moba_attention(qk, p2, slopes, batch, seq)
    yb = sb_attention(p2, batch, seq)
    merged = merge_branches(ya, yb, w_bm, w_bs, gates)
    x = matmul(_mm_residual_kernel, merged, w_out, [(x, "tile")], n=D_MODEL, col_off=0,
               out_dtype=F32, tm=1024, tn=512, name="out_proj")
    h2 = rms_norm(x, g_mlp)
    u = matmul(_mm_relu2_kernel, h2, w_up, [], n=D_FF, col_off=0,
               out_dtype=BF16, tm=1024, tn=1024, name="mlp_up")
    return matmul_kgrid_residual(u, w_down, x, tm=1024, tn=1024, tk=2048, name="mlp_down")


def kernel(x, norm_mix, w_in, b_gate, q_norm, k_norm, w_branch_moba, w_branch_sb, w_out,
           norm_mlp, w_up, w_down):
    batch, seq, d = x.shape
    depth = w_in.shape[0]
    slopes = jnp.exp2(-8.0 * jnp.arange(1, N_HEADS + 1, dtype=F32) / N_HEADS)
    y = x.reshape(batch * seq, d)
    for l in range(depth):
        y = _layer(y, norm_mix[l], layer_weight_bf16(w_in, l), b_gate[l], q_norm[l], k_norm[l],
                   layer_weight_bf16(w_branch_moba, l), layer_weight_bf16(w_branch_sb, l),
                   layer_weight_bf16(w_out, l), norm_mlp[l], layer_weight_bf16(w_up, l),
                   layer_weight_bf16(w_down, l), slopes, batch, seq)
    return y.reshape(batch, seq, d)
```

```python
import functools

import jax
import jax.numpy as jnp
from jax import lax
from jax.experimental import pallas as pl
from jax.experimental.pallas import tpu as pltpu

D_MODEL = 4096
HEAD_DIM = 128
N_HEADS = 16
WIDTH = N_HEADS * HEAD_DIM
MOBA_BLOCK = 256
MOBA_TOPK = 3
D_FF = 4 * D_MODEL
RMS_EPS = 1e-6
NEG = -1e30
SCALE = HEAD_DIM ** -0.5

VMEM_LIMIT_BYTES = 56 * 1024 * 1024

F32 = jnp.float32
BF16 = jnp.bfloat16


def _params(semantics):
    return pltpu.CompilerParams(dimension_semantics=semantics, vmem_limit_bytes=VMEM_LIMIT_BYTES)


def _rms_kernel(x_ref, g_ref, o_ref):
    x = x_ref[...]
    ms = jnp.mean(x * x, axis=-1, keepdims=True)
    o_ref[...] = ((x * lax.rsqrt(ms + RMS_EPS)) * g_ref[...]).astype(o_ref.dtype)


def rms_norm(x, g, *, tm=256):
    m, d = x.shape
    return pl.pallas_call(
        _rms_kernel,
        grid=(m // tm,),
        in_specs=[pl.BlockSpec((tm, d), lambda i: (i, 0)),
                  pl.BlockSpec((1, d), lambda i: (0, 0))],
        out_specs=pl.BlockSpec((tm, d), lambda i: (i, 0)),
        out_shape=jax.ShapeDtypeStruct((m, d), BF16),
        compiler_params=_params(("arbitrary",)),
        name="rms_norm",
    )(x, g.reshape(1, d))


def _cast_kernel(w_ref, o_ref):
    o_ref[...] = w_ref[...].astype(o_ref.dtype)


def layer_weight_bf16(w, layer, *, tr=512, tc=4096):
    _, rows, cols = w.shape
    tc = min(tc, cols)
    return pl.pallas_call(
        _cast_kernel,
        grid=(rows // tr, cols // tc),
        in_specs=[pl.BlockSpec((None, tr, tc), lambda i, j: (layer, i, j))],
        out_specs=pl.BlockSpec((tr, tc), lambda i, j: (i, j)),
        out_shape=jax.ShapeDtypeStruct((rows, cols), BF16),
        compiler_params=_params(("arbitrary", "arbitrary")),
        name="weight_cast",
    )(w)


def _mm_colscale_kernel(a_ref, w_ref, s_ref, o_ref):
    acc = jnp.dot(a_ref[...], w_ref[...], preferred_element_type=F32)
    o_ref[...] = (acc * s_ref[...]).astype(o_ref.dtype)


def _mm_headnorm_kernel(a_ref, w_ref, g_ref, o_ref):
    acc = jnp.dot(a_ref[...], w_ref[...], preferred_element_type=F32)
    for c in range(acc.shape[1] // HEAD_DIM):
        sl = slice(c * HEAD_DIM, (c + 1) * HEAD_DIM)
        y = acc[:, sl]
        ms = jnp.mean(y * y, axis=-1, keepdims=True)
        o_ref[:, sl] = ((y * lax.rsqrt(ms + RMS_EPS)) * g_ref[:, sl]).astype(o_ref.dtype)


def _mm_sigmoid_kernel(a_ref, w_ref, b_ref, o_ref):
    acc = jnp.dot(a_ref[...], w_ref[...], preferred_element_type=F32)
    o_ref[...] = jax.nn.sigmoid(acc + b_ref[...]).astype(o_ref.dtype)


def _mm_relu2_kernel(a_ref, w_ref, o_ref):
    acc = jnp.dot(a_ref[...], w_ref[...], preferred_element_type=F32)
    o_ref[...] = jnp.square(jnp.maximum(acc, 0.0)).astype(o_ref.dtype)


def _mm_residual_kernel(a_ref, w_ref, r_ref, o_ref):
    acc = jnp.dot(a_ref[...], w_ref[...], preferred_element_type=F32)
    o_ref[...] = r_ref[...] + acc


def matmul(kernel, a, w, extras, *, n, col_off, out_dtype, tm, tn, name):
    m, k = a.shape
    off = col_off // tn
    assert off * tn == col_off and n % tn == 0 and m % tm == 0
    in_specs = [pl.BlockSpec((tm, k), lambda i, j: (i, 0)),
                pl.BlockSpec((k, tn), lambda i, j: (0, j + off))]
    args = [a, w]
    for arr, kind in extras:
        if kind == "row":
            in_specs.append(pl.BlockSpec((1, tn), lambda i, j: (0, j)))
        else:
            in_specs.append(pl.BlockSpec((tm, tn), lambda i, j: (i, j)))
        args.append(arr)
    return pl.pallas_call(
        kernel,
        grid=(m // tm, n // tn),
        in_specs=in_specs,
        out_specs=pl.BlockSpec((tm, tn), lambda i, j: (i, j)),
        out_shape=jax.ShapeDtypeStruct((m, n), out_dtype),
        compiler_params=_params(("arbitrary", "arbitrary")),
        name=name,
    )(*args)


def _mm_kgrid_residual_kernel(a_ref, w_ref, r_ref, o_ref, acc_ref):
    kk = pl.program_id(2)

    @pl.when(kk == 0)
    def _():
        acc_ref[...] = jnp.zeros_like(acc_ref)

    acc_ref[...] += jnp.dot(a_ref[...], w_ref[...], preferred_element_type=F32)

    @pl.when(kk == pl.num_programs(2) - 1)
    def _():
        o_ref[...] = r_ref[...] + acc_ref[...]


def matmul_kgrid_residual(a, w, r, *, tm, tn, tk, name):
    m, k = a.shape
    n = w.shape[1]
    return pl.pallas_call(
        _mm_kgrid_residual_kernel,
        grid=(m // tm, n // tn, k // tk),
        in_specs=[pl.BlockSpec((tm, tk), lambda i, j, kk: (i, kk)),
                  pl.BlockSpec((tk, tn), lambda i, j, kk: (kk, j)),
                  pl.BlockSpec((tm, tn), lambda i, j, kk: (i, j))],
        out_specs=pl.BlockSpec((tm, tn), lambda i, j, kk: (i, j)),
        out_shape=jax.ShapeDtypeStruct((m, n), F32),
        scratch_shapes=[pltpu.VMEM((tm, tn), F32)],
        compiler_params=_params(("arbitrary", "arbitrary", "arbitrary")),
        name=name,
    )(a, w, r)


def _merge_kernel(ya_ref, yb_ref, wa_ref, wb_ref, ga_ref, gb_ref, o_ref):
    pa = jnp.dot(ya_ref[...], wa_ref[...], preferred_element_type=F32)
    pb = jnp.dot(yb_ref[...], wb_ref[...], preferred_element_type=F32)
    o_ref[...] = (ga_ref[...] * pa + gb_ref[...] * pb).astype(o_ref.dtype)


def merge_branches(ya, yb, wa, wb, gates, *, tm=1024, tn=512):
    m, k = ya.shape
    n = wa.shape[1]
    goff = n // tn
    return pl.pallas_call(
        _merge_kernel,
        grid=(m // tm, n // tn),
        in_specs=[pl.BlockSpec((tm, k), lambda i, j: (i, 0)),
                  pl.BlockSpec((tm, k), lambda i, j: (i, 0)),
                  pl.BlockSpec((k, tn), lambda i, j: (0, j)),
                  pl.BlockSpec((k, tn), lambda i, j: (0, j)),
                  pl.BlockSpec((tm, tn), lambda i, j: (i, j)),
                  pl.BlockSpec((tm, tn), lambda i, j: (i, j + goff))],
        out_specs=pl.BlockSpec((tm, tn), lambda i, j: (i, j)),
        out_shape=jax.ShapeDtypeStruct((m, n), BF16),
        compiler_params=_params(("arbitrary", "arbitrary")),
        name="merge_branches",
    )(ya, yb, wa, wb, gates, gates)


ATT_TILE = 256
HEADS_PER_STEP = 4
GROUP_WIDTH = HEADS_PER_STEP * HEAD_DIM
N_GROUPS = N_HEADS // HEADS_PER_STEP

LOG2E = 1.4426950408889634
INV_LN2 = LOG2E
SCORE_LOG2 = SCALE * LOG2E
ZERO_WEIGHT_LOG2 = -160.0
VT_ROWS = HEAD_DIM + 16


def _head_cols(g):
    return slice(g * HEAD_DIM, (g + 1) * HEAD_DIM)


def _transpose_values(v_ref, vt_ref, g):
    n_chunks = v_ref.shape[0] // ATT_TILE

    def body(c, carry):
        r0 = pl.multiple_of(c * ATT_TILE, ATT_TILE)
        vt_ref[g, :HEAD_DIM, pl.ds(r0, ATT_TILE)] = (
            v_ref[pl.ds(r0, ATT_TILE), _head_cols(g)].astype(F32).T.astype(BF16))
        return carry

    lax.fori_loop(0, n_chunks, body, 0, unroll=8)


def _nt_dot(a, b):
    return lax.dot_general(a, b, (((1,), (1,)), ((), ())), preferred_element_type=F32)


def _neg_abs(x):
    bits = lax.bitcast_convert_type(x, jnp.uint32) | jnp.uint32(0x80000000)
    return lax.bitcast_convert_type(bits, F32)


SB_TILES = 2


def _sb_kernel(q_ref, k_ref, v_ref, o_ref, vt_ref, acc_ref):
    ip = pl.program_id(2)
    t = ATT_TILE
    tasks = [(a, g) for a in range(SB_TILES) for g in range(HEADS_PER_STEP)]

    @pl.when(ip == 0)
    def _():
        for g in range(HEADS_PER_STEP):
            _transpose_values(v_ref, vt_ref, g)

    row = lax.broadcasted_iota(jnp.int32, (t, t), 0)
    col = lax.broadcasted_iota(jnp.int32, (t, t), 1)
    upper = (col > row).astype(BF16)

    def block(blocks, carries, diagonal):
        starts = [pl.multiple_of(jnp.maximum(j, 0) * t, t) for j in blocks]
        zs = [_nt_dot(k_ref[pl.ds(starts[a], t), _head_cols(g)],
                      q_ref[a * t:(a + 1) * t, _head_cols(g)]) for a, g in tasks]
        log_betas, log_1ms, cums = [], [], []
        for n in range(len(tasks)):
            z = zs[n]
            log_beta = jnp.minimum(z, 0.0) - jnp.log2(1.0 + jnp.exp2(_neg_abs(z)))
            log_1m = log_beta - z
            if diagonal:
                log_1m = jnp.where(row < col, log_1m, 0.0)
            hi = log_1m.astype(BF16)
            lo = (log_1m - hi.astype(F32)).astype(BF16)
            cums.append(jnp.dot(upper, hi, preferred_element_type=F32)
                        + jnp.dot(upper, lo, preferred_element_type=F32))
            log_betas.append(log_beta)
            log_1ms.append(log_1m)
        pvs, new_carries = [], []
        for n, (a, g) in enumerate(tasks):
            w = jnp.exp2(log_betas[n] + (cums[n] + carries[n]))
            if diagonal:
                w = jnp.where(row < col, w, 0.0)
            pvs.append(jnp.dot(vt_ref[g, :, pl.ds(starts[a], t)], w.astype(BF16),
                               preferred_element_type=F32))
            carry = carries[n] + jnp.sum(log_1ms[n], axis=0, keepdims=True)
            new_carries.append(jnp.where(blocks[a] <= 0, NEG, carry))
        return new_carries, pvs

    def any_alive(carries):
        top = carries[0]
        for c in carries[1:]:
            top = jnp.maximum(top, c)
        return jnp.max(top, axis=1, keepdims=True)[0, 0] > ZERO_WEIGHT_LOG2

    first = [SB_TILES * ip + a for a in range(SB_TILES)]
    carries, pvs = block(first, [jnp.zeros((1, t), F32) for _ in tasks], True)
    for n in range(len(tasks)):
        acc_ref[n] = pvs[n]

    def cond(state):
        step, alive = state[0], state[1]
        return jnp.logical_and(step < first[-1], alive)

    def body(state):
        step = state[0]
        carries, pvs = block([j - 1 - step for j in first], list(state[2:]), False)
        for n in range(len(tasks)):
            acc_ref[n] += pvs[n]
        return (step + 1, any_alive(carries), *carries)

    lax.while_loop(cond, body, (jnp.int32(0), any_alive(carries), *carries))
    for n, (a, g) in enumerate(tasks):
        o_ref[a * t:(a + 1) * t, _head_cols(g)] = acc_ref[n].T.astype(o_ref.dtype)


def sb_attention(p2, batch, seq):
    t = ATT_TILE
    rows = SB_TILES * t
    nq = seq // rows
    qc, kc, vc = N_GROUPS, 2 * N_GROUPS, 3 * N_GROUPS
    return pl.pallas_call(
        _sb_kernel,
        grid=(batch, N_GROUPS, nq),
        in_specs=[pl.BlockSpec((rows, GROUP_WIDTH), lambda b, h, i: (b * nq + i, qc + h)),
                  pl.BlockSpec((seq, GROUP_WIDTH), lambda b, h, i: (b, kc + h)),
                  pl.BlockSpec((seq, GROUP_WIDTH), lambda b, h, i: (b, vc + h))],
        out_specs=pl.BlockSpec((rows, GROUP_WIDTH), lambda b, h, i: (b * nq + i, h)),
        out_shape=jax.ShapeDtypeStruct((batch * seq, WIDTH), BF16),
        scratch_shapes=[pltpu.VMEM((HEADS_PER_STEP, HEAD_DIM, seq), BF16),
                        pltpu.VMEM((SB_TILES * HEADS_PER_STEP, HEAD_DIM, t), F32)],
        compiler_params=_params(("arbitrary", "arbitrary", "arbitrary")),
        name="sb_attention",
    )(p2, p2, p2)


def _moba_kernel(slopes_ref, inv_span_ref, q_ref, k_ref, v_ref, o_ref, vt_ref, kmh_ref, kml_ref,
                 kbias_ref, selb_ref, acc_ref, knorm_ref, *p_refs, n_blocks):
    hg = pl.program_id(1)
    i = pl.program_id(2)
    t = ATT_TILE
    heads = range(HEADS_PER_STEP)
    n_heads = HEADS_PER_STEP
    seq = k_ref.shape[0]

    row = lax.broadcasted_iota(jnp.int32, (t, t), 0)
    col = lax.broadcasted_iota(jnp.int32, (t, t), 1)
    slope_log2 = [slopes_ref[hg * HEADS_PER_STEP + g] * LOG2E for g in heads]
    inv_span = [inv_span_ref[hg * HEADS_PER_STEP + g] for g in heads]

    @pl.when(i == 0)
    def _():
        for g in heads:
            _transpose_values(v_ref, vt_ref, g)
            vt_ref[g, HEAD_DIM:, :] = jnp.ones((VT_ROWS - HEAD_DIM, seq + t), BF16)
            vt_ref[g, :HEAD_DIM, pl.ds(seq, t)] = jnp.zeros((HEAD_DIM, t), BF16)
            k32 = k_ref[:, _head_cols(g)].astype(F32)
            kn2 = jnp.max(jnp.sum(k32 * k32, axis=1, keepdims=True), axis=0, keepdims=True)
            knorm_ref[g] = jnp.broadcast_to(kn2, knorm_ref.shape[1:])
            km = jnp.mean(k32.reshape(n_blocks, t, HEAD_DIM), axis=1)
            hi = km.astype(BF16)
            kmh_ref[g] = hi
            kml_ref[g] = (km - hi.astype(F32)).astype(BF16)
            kbias_ref[g] = row.astype(F32) * slope_log2[g]

    blk = lax.broadcasted_iota(jnp.int32, (n_blocks, t), 0)
    valid = blk < i

    def scores(j, g):
        k0 = pl.multiple_of(j * t, t)
        return _nt_dot(k_ref[pl.ds(k0, t), _head_cols(g)], q_ref[:, _head_cols(g)])

    def weighted_sum(j, g, slot, alpha):
        k0 = pl.multiple_of(j * t, t)
        k1 = pl.multiple_of((j + 1) * t, t)
        acc_ref[g] = (alpha * acc_ref[g]
                      + jnp.dot(vt_ref[g, :, pl.ds(k0, t)], p_refs[g][slot, 0],
                                preferred_element_type=F32)
                      + jnp.dot(vt_ref[g, :, pl.ds(k1, t)], p_refs[g][slot, 1],
                                preferred_element_type=F32))

    ms, own_ps, keeps = [], [], []
    for g in heads:
        q = q_ref[:, _head_cols(g)]
        gate = _nt_dot(kmh_ref[g], q) + _nt_dot(kml_ref[g], q)
        gate = jnp.where(valid, gate, NEG)
        chosen = jnp.zeros((n_blocks, t), jnp.bool_)
        for _ in range(MOBA_TOPK):
            top = jnp.max(gate, axis=0, keepdims=True)
            first = jnp.min(jnp.where(gate == top, blk, n_blocks), axis=0, keepdims=True)
            pick = blk == first
            chosen = jnp.logical_or(chosen, pick)
            gate = jnp.where(pick, -jnp.inf, gate)
        selb_ref[g] = jnp.where(jnp.logical_and(chosen, valid), 0.0, NEG)

        x = jnp.where(col >= row, scores(i, g) + kbias_ref[g], NEG)
        m = jnp.max(x, axis=0, keepdims=True)
        ms.append(m)
        own_ps.append(jnp.exp2(x - m).astype(BF16))
        acc_ref[g] = jnp.zeros((VT_ROWS, t), F32)

        q32 = q.astype(F32)
        qn2 = jnp.max(jnp.sum(q32 * q32, axis=1, keepdims=True), axis=0, keepdims=True)
        reach = jnp.sqrt(qn2 * knorm_ref[g, :1, :1]) - jnp.min(m, axis=1, keepdims=True)
        keeps.append((reach - ZERO_WEIGHT_LOG2) * inv_span[g] + 2.0)

    keep = keeps[0]
    for other in keeps[1:]:
        keep = jnp.maximum(keep, other)
    keep = jnp.clip(keep, 0.0, float(n_blocks)).astype(jnp.int32)[0, 0]
    first_step = jnp.maximum(i - keep, 0) >> 1
    for g in heads:
        p_refs[g][first_step & 1, 0] = own_ps[g]
        p_refs[g][first_step & 1, 1] = jnp.zeros((t, t), BF16)

    def body(step, state):
        prev, state = state[0], state[1:]
        alphas, ms = state[:n_heads], state[n_heads:]
        slot = step & 1
        j = 2 * step
        for g in heads:
            weighted_sum(prev, g, slot, alphas[g])
        dots = [(scores(j, g), scores(j + 1, g)) for g in heads]
        new_alphas, new_ms = [], []
        for g in heads:
            x0 = dots[g][0] + kbias_ref[g]
            x1 = dots[g][1] + kbias_ref[g]
            base = slope_log2[g] * ((j - i) * t).astype(F32)
            shift0 = base + selb_ref[g, pl.ds(j, 1), :]
            shift1 = (base + slope_log2[g] * t) + selb_ref[g, pl.ds(j + 1, 1), :]
            m_new = jnp.maximum(ms[g], jnp.maximum(jnp.max(x0, axis=0, keepdims=True) + shift0,
                                                   jnp.max(x1, axis=0, keepdims=True) + shift1))
            new_alphas.append(jnp.exp2(ms[g] - m_new))
            p_refs[g][1 - slot, 0] = jnp.exp2(x0 - (m_new - shift0)).astype(BF16)
            p_refs[g][1 - slot, 1] = jnp.exp2(x1 - (m_new - shift1)).astype(BF16)
            new_ms.append(m_new)
        return (j, *new_alphas, *new_ms)

    n_steps = (i + 1) >> 1
    ones = [jnp.ones((1, t), F32) for _ in heads]
    state = lax.fori_loop(first_step, n_steps, body, (i, *ones, *ms))
    prev, alphas = state[0], state[1:1 + n_heads]
    for g in heads:
        weighted_sum(prev, g, n_steps & 1, alphas[g])
        acc = acc_ref[g]
        o_ref[:, _head_cols(g)] = (acc[:HEAD_DIM] / acc[HEAD_DIM:HEAD_DIM + 1]).T.astype(o_ref.dtype)


def moba_attention(qk, p2, slopes, batch, seq):
    t = ATT_TILE
    nq = seq // t
    n_blocks = seq // MOBA_BLOCK
    grid_spec = pltpu.PrefetchScalarGridSpec(
        num_scalar_prefetch=2,
        grid=(batch, N_GROUPS, nq),
        in_specs=[pl.BlockSpec((t, GROUP_WIDTH), lambda b, h, i, s, r: (b * nq + i, h)),
                  pl.BlockSpec((seq, GROUP_WIDTH), lambda b, h, i, s, r: (b, N_GROUPS + h)),
                  pl.BlockSpec((seq, GROUP_WIDTH), lambda b, h, i, s, r: (b, h))],
        out_specs=pl.BlockSpec((t, GROUP_WIDTH), lambda b, h, i, s, r: (b * nq + i, h)),
        scratch_shapes=[pltpu.VMEM((HEADS_PER_STEP, VT_ROWS, seq + t), BF16),
                        pltpu.VMEM((HEADS_PER_STEP, n_blocks, HEAD_DIM), BF16),
                        pltpu.VMEM((HEADS_PER_STEP, n_blocks, HEAD_DIM), BF16),
                        pltpu.VMEM((HEADS_PER_STEP, t, t), F32),
                        pltpu.VMEM((HEADS_PER_STEP, n_blocks, t), F32),
                        pltpu.VMEM((HEADS_PER_STEP, VT_ROWS, t), F32),
                        pltpu.VMEM((HEADS_PER_STEP, 8, HEAD_DIM), F32)]
                       + [pltpu.VMEM((2, 2, t, t), BF16) for _ in range(HEADS_PER_STEP)],
    )
    return pl.pallas_call(
        functools.partial(_moba_kernel, n_blocks=n_blocks),
        grid_spec=grid_spec,
        out_shape=jax.ShapeDtypeStruct((batch * seq, WIDTH), BF16),
        compiler_params=_params(("arbitrary", "arbitrary", "arbitrary")),
        name="moba_attention",
    )(slopes, 1.0 / (slopes * (LOG2E * MOBA_BLOCK)), qk, qk, p2)


def _layer(x, g_mix, w_in, b_gate, g_q, g_k, w_bm, w_bs, w_out, g_mlp, w_up, w_down, slopes,
           batch, seq):
    h = rms_norm(x, g_mix)
    qk_gain = jnp.concatenate([jnp.tile(g_q * SCORE_LOG2, N_HEADS),
                               jnp.tile(g_k, N_HEADS)]).reshape(1, 2 * WIDTH)
    qk = matmul(_mm_headnorm_kernel, h, w_in, [(qk_gain, "row")], n=2 * WIDTH, col_off=0,
                out_dtype=BF16, tm=1024, tn=1024, name="in_proj_qk_moba")
    rest_scale = jnp.concatenate([jnp.ones((WIDTH,), F32), jnp.full((WIDTH,), SCORE_LOG2, F32),
                                  jnp.ones((2 * WIDTH,), F32)]).reshape(1, 4 * WIDTH)
    p2 = matmul(_mm_colscale_kernel, h, w_in, [(rest_scale, "row")], n=4 * WIDTH, col_off=2 * WIDTH,
                out_dtype=BF16, tm=1024, tn=1024, name="in_proj_rest")
    gates = matmul(_mm_sigmoid_kernel, h, w_in, [(b_gate.reshape(1, -1), "row")], n=2 * D_MODEL,
                   col_off=6 * WIDTH, out_dtype=F32, tm=1024, tn=1024, name="in_proj_gates")
    ya = moba_attention(qk, p2, slopes, batch, seq)
    yb = sb_attention(p2, batch, seq)
    merged = merge_branches(ya, yb, w_bm, w_bs, gates)
    x = matmul(_mm_residual_kernel, merged, w_out, [(x, "tile")], n=D_MODEL, col_off=0,
               out_dtype=F32, tm=1024, tn=512, name="out_proj")
    h2 = rms_norm(x, g_mlp)
    u = matmul(_mm_relu2_kernel, h2, w_up, [], n=D_FF, col_off=0,
               out_dtype=BF16, tm=1024, tn=1024, name="mlp_up")
    return matmul_kgrid_residual(u, w_down, x, tm=1024, tn=1024, tk=2048, name="mlp_down")


def kernel(x, norm_mix, w_in, b_gate, q_norm, k_norm, w_branch_moba, w_branch_sb, w_out,
           norm_mlp, w_up, w_down):
    batch, seq, d = x.shape
    depth = w_in.shape[0]
    slopes = jnp.exp2(-8.0 * jnp.arange(1, N_HEADS + 1, dtype=F32) / N_HEADS)
    y = x.reshape(batch * seq, d)
    for l in range(depth):
        y = _layer(y, norm_mix[l], layer_weight_bf16(w_in, l), b_gate[l], q_norm[l], k_norm[l],
                   layer_weight_bf16(w_branch_moba, l), layer_weight_bf16(w_branch_sb, l),
                   layer_weight_bf16(w_out, l), norm_mlp[l], layer_weight_bf16(w_up, l),
                   layer_weight_bf16(w_down, l), slopes, batch, seq)
    return y.reshape(batch, seq, d)
```

```python
import functools

import jax
import jax.numpy as jnp
from jax import lax
from jax.experimental import pallas as pl
from jax.experimental.pallas import tpu as pltpu

D_MODEL = 4096
HEAD_DIM = 128
N_HEADS = 16
WIDTH = N_HEADS * HEAD_DIM
MOBA_BLOCK = 256
MOBA_TOPK = 3
D_FF = 4 * D_MODEL
RMS_EPS = 1e-6
NEG = -1e30
SCALE = HEAD_DIM ** -0.5

VMEM_LIMIT_BYTES = 56 * 1024 * 1024

F32 = jnp.float32
BF16 = jnp.bfloat16


def _params(semantics):
    return pltpu.CompilerParams(dimension_semantics=semantics, vmem_limit_bytes=VMEM_LIMIT_BYTES)


def _rms_kernel(x_ref, g_ref, o_ref):
    x = x_ref[...]
    ms = jnp.mean(x * x, axis=-1, keepdims=True)
    o_ref[...] = ((x * lax.rsqrt(ms + RMS_EPS)) * g_ref[...]).astype(o_ref.dtype)


def rms_norm(x, g, *, tm=256):
    m, d = x.shape
    return pl.pallas_call(
        _rms_kernel,
        grid=(m // tm,),
        in_specs=[pl.BlockSpec((tm, d), lambda i: (i, 0)),
                  pl.BlockSpec((1, d), lambda i: (0, 0))],
        out_specs=pl.BlockSpec((tm, d), lambda i: (i, 0)),
        out_shape=jax.ShapeDtypeStruct((m, d), BF16),
        compiler_params=_params(("arbitrary",)),
        name="rms_norm",
    )(x, g.reshape(1, d))


def _cast_kernel(w_ref, o_ref):
    o_ref[...] = w_ref[...].astype(o_ref.dtype)


def layer_weight_bf16(w, layer, *, tr=512, tc=4096):
    _, rows, cols = w.shape
    tc = min(tc, cols)
    return pl.pallas_call(
        _cast_kernel,
        grid=(rows // tr, cols // tc),
        in_specs=[pl.BlockSpec((None, tr, tc), lambda i, j: (layer, i, j))],
        out_specs=pl.BlockSpec((tr, tc), lambda i, j: (i, j)),
        out_shape=jax.ShapeDtypeStruct((rows, cols), BF16),
        compiler_params=_params(("arbitrary", "arbitrary")),
        name="weight_cast",
    )(w)


def _mm_colscale_kernel(a_ref, w_ref, s_ref, o_ref):
    acc = jnp.dot(a_ref[...], w_ref[...], preferred_element_type=F32)
    o_ref[...] = (acc * s_ref[...]).astype(o_ref.dtype)


def _mm_headnorm_kernel(a_ref, w_ref, g_ref, o_ref):
    acc = jnp.dot(a_ref[...], w_ref[...], preferred_element_type=F32)
    for c in range(acc.shape[1] // HEAD_DIM):
        sl = slice(c * HEAD_DIM, (c + 1) * HEAD_DIM)
        y = acc[:, sl]
        ms = jnp.mean(y * y, axis=-1, keepdims=True)
        o_ref[:, sl] = ((y * lax.rsqrt(ms + RMS_EPS)) * g_ref[:, sl]).astype(o_ref.dtype)


def _mm_sigmoid_kernel(a_ref, w_ref, b_ref, o_ref):
    acc = jnp.dot(a_ref[...], w_ref[...], preferred_element_type=F32)
    o_ref[...] = jax.nn.sigmoid(acc + b_ref[...]).astype(o_ref.dtype)


def _mm_relu2_kernel(a_ref, w_ref, o_ref):
    acc = jnp.dot(a_ref[...], w_ref[...], preferred_element_type=F32)
    o_ref[...] = jnp.square(jnp.maximum(acc, 0.0)).astype(o_ref.dtype)


def _mm_residual_kernel(a_ref, w_ref, r_ref, o_ref):
    acc = jnp.dot(a_ref[...], w_ref[...], preferred_element_type=F32)
    o_ref[...] = r_ref[...] + acc


def _with_side_casts(kernel, n_in, n_side, *refs):
    ins, side_ins = refs[:n_in], refs[n_in:n_in + n_side]
    out, side_outs = refs[n_in + n_side], refs[n_in + n_side + 1:]
    kernel(*ins, out)
    for src, dst in zip(side_ins, side_outs):
        dst[...] = src[...].astype(dst.dtype)


def _side_cast_specs(side_casts, grid):
    gi, gj = grid
    in_specs, out_specs, out_shapes, arrays = [], [], [], []
    for w, layer in side_casts:
        _, rows, cols = w.shape
        br, bc = rows // gi, cols // gj
        assert br * gi == rows and bc * gj == cols and br % 16 == 0 and bc % 128 == 0
        in_specs.append(pl.BlockSpec((None, br, bc), lambda i, j, layer=layer: (layer, i, j)))
        out_specs.append(pl.BlockSpec((br, bc), lambda i, j: (i, j)))
        out_shapes.append(jax.ShapeDtypeStruct((rows, cols), BF16))
        arrays.append(w)
    return in_specs, out_specs, out_shapes, arrays


def matmul(kernel, a, w, extras, *, n, col_off, out_dtype, tm, tn, name, side_casts=()):
    m, k = a.shape
    off = col_off // tn
    assert off * tn == col_off and n % tn == 0 and m % tm == 0
    grid = (m // tm, n // tn)
    in_specs = [pl.BlockSpec((tm, k), lambda i, j: (i, 0)),
                pl.BlockSpec((k, tn), lambda i, j: (0, j + off))]
    args = [a, w]
    for arr, kind in extras:
        if kind == "row":
            in_specs.append(pl.BlockSpec((1, tn), lambda i, j: (0, j)))
        else:
            in_specs.append(pl.BlockSpec((tm, tn), lambda i, j: (i, j)))
        args.append(arr)
    side_in, side_out, side_shapes, side_arrays = _side_cast_specs(side_casts, grid)
    outs = pl.pallas_call(
        functools.partial(_with_side_casts, kernel, len(args), len(side_arrays)),
        grid=grid,
        in_specs=in_specs + side_in,
        out_specs=[pl.BlockSpec((tm, tn), lambda i, j: (i, j))] + side_out,
        out_shape=[jax.ShapeDtypeStruct((m, n), out_dtype)] + side_shapes,
        compiler_params=_params(("arbitrary", "arbitrary")),
        name=name,
    )(*args, *side_arrays)
    return outs if side_casts else outs[0]


def _mm_kgrid_residual_kernel(a_ref, w_ref, r_ref, o_ref, acc_ref):
    kk = pl.program_id(2)

    @pl.when(kk == 0)
    def _():
        acc_ref[...] = jnp.zeros_like(acc_ref)

    acc_ref[...] += jnp.dot(a_ref[...], w_ref[...], preferred_element_type=F32)

    @pl.when(kk == pl.num_programs(2) - 1)
    def _():
        o_ref[...] = r_ref[...] + acc_ref[...]


def matmul_kgrid_residual(a, w, r, *, tm, tn, tk, name):
    m, k = a.shape
    n = w.shape[1]
    return pl.pallas_call(
        _mm_kgrid_residual_kernel,
        grid=(m // tm, n // tn, k // tk),
        in_specs=[pl.BlockSpec((tm, tk), lambda i, j, kk: (i, kk)),
                  pl.BlockSpec((tk, tn), lambda i, j, kk: (kk, j)),
                  pl.BlockSpec((tm, tn), lambda i, j, kk: (i, j))],
        out_specs=pl.BlockSpec((tm, tn), lambda i, j, kk: (i, j)),
        out_shape=jax.ShapeDtypeStruct((m, n), F32),
        scratch_shapes=[pltpu.VMEM((tm, tn), F32)],
        compiler_params=_params(("arbitrary", "arbitrary", "arbitrary")),
        name=name,
    )(a, w, r)


def _merge_kernel(ya_ref, yb_ref, wa_ref, wb_ref, ga_ref, gb_ref, o_ref):
    pa = jnp.dot(ya_ref[...], wa_ref[...], preferred_element_type=F32)
    pb = jnp.dot(yb_ref[...], wb_ref[...], preferred_element_type=F32)
    o_ref[...] = (ga_ref[...] * pa + gb_ref[...] * pb).astype(o_ref.dtype)


def merge_branches(ya, yb, wa, wb, gates, *, tm=1024, tn=512, side_casts=()):
    m, k = ya.shape
    n = wa.shape[1]
    goff = n // tn
    grid = (m // tm, n // tn)
    side_in, side_out, side_shapes, side_arrays = _side_cast_specs(side_casts, grid)
    outs = pl.pallas_call(
        functools.partial(_with_side_casts, _merge_kernel, 6, len(side_arrays)),
        grid=grid,
        in_specs=[pl.BlockSpec((tm, k), lambda i, j: (i, 0)),
                  pl.BlockSpec((tm, k), lambda i, j: (i, 0)),
                  pl.BlockSpec((k, tn), lambda i, j: (0, j)),
                  pl.BlockSpec((k, tn), lambda i, j: (0, j)),
                  pl.BlockSpec((tm, tn), lambda i, j: (i, j)),
                  pl.BlockSpec((tm, tn), lambda i, j: (i, j + goff))] + side_in,
        out_specs=[pl.BlockSpec((tm, tn), lambda i, j: (i, j))] + side_out,
        out_shape=[jax.ShapeDtypeStruct((m, n), BF16)] + side_shapes,
        compiler_params=_params(("arbitrary", "arbitrary")),
        name="merge_branches",
    )(ya, yb, wa, wb, gates, gates, *side_arrays)
    return outs if side_casts else outs[0]


ATT_TILE = 256
HEADS_PER_STEP = 4
GROUP_WIDTH = HEADS_PER_STEP * HEAD_DIM
N_GROUPS = N_HEADS // HEADS_PER_STEP

LOG2E = 1.4426950408889634
INV_LN2 = LOG2E
SCORE_LOG2 = SCALE * LOG2E
ZERO_WEIGHT_LOG2 = -160.0
VT_ROWS = HEAD_DIM + 16


def _head_cols(g):
    return slice(g * HEAD_DIM, (g + 1) * HEAD_DIM)


def _transpose_values(v_ref, vt_ref, g):
    n_chunks = v_ref.shape[0] // ATT_TILE

    def body(c, carry):
        r0 = pl.multiple_of(c * ATT_TILE, ATT_TILE)
        vt_ref[g, :HEAD_DIM, pl.ds(r0, ATT_TILE)] = (
            v_ref[pl.ds(r0, ATT_TILE), _head_cols(g)].astype(F32).T.astype(BF16))
        return carry

    lax.fori_loop(0, n_chunks, body, 0, unroll=8)


def _nt_dot(a, b):
    return lax.dot_general(a, b, (((1,), (1,)), ((), ())), preferred_element_type=F32)


def _neg_abs(x):
    bits = lax.bitcast_convert_type(x, jnp.uint32) | jnp.uint32(0x80000000)
    return lax.bitcast_convert_type(bits, F32)


SB_TILES = 2


def _sb_kernel(q_ref, k_ref, v_ref, o_ref, vt_ref, acc_ref):
    ip = pl.program_id(2)
    t = ATT_TILE
    tasks = [(a, g) for a in range(SB_TILES) for g in range(HEADS_PER_STEP)]

    @pl.when(ip == 0)
    def _():
        for g in range(HEADS_PER_STEP):
            _transpose_values(v_ref, vt_ref, g)

    row = lax.broadcasted_iota(jnp.int32, (t, t), 0)
    col = lax.broadcasted_iota(jnp.int32, (t, t), 1)
    upper = (col > row).astype(BF16)

    def block(blocks, carries, diagonal):
        starts = [pl.multiple_of(jnp.maximum(j, 0) * t, t) for j in blocks]
        zs = [_nt_dot(k_ref[pl.ds(starts[a], t), _head_cols(g)],
                      q_ref[a * t:(a + 1) * t, _head_cols(g)]) for a, g in tasks]
        log_betas, log_1ms, cums = [], [], []
        for n in range(len(tasks)):
            z = zs[n]
            log_beta = jnp.minimum(z, 0.0) - jnp.log2(1.0 + jnp.exp2(_neg_abs(z)))
            log_1m = log_beta - z
            if diagonal:
                log_1m = jnp.where(row < col, log_1m, 0.0)
            hi = log_1m.astype(BF16)
            lo = (log_1m - hi.astype(F32)).astype(BF16)
            cums.append(jnp.dot(upper, hi, preferred_element_type=F32)
                        + jnp.dot(upper, lo, preferred_element_type=F32))
            log_betas.append(log_beta)
            log_1ms.append(log_1m)
        pvs, new_carries = [], []
        for n, (a, g) in enumerate(tasks):
            w = jnp.exp2(log_betas[n] + (cums[n] + carries[n]))
            if diagonal:
                w = jnp.where(row < col, w, 0.0)
            pvs.append(jnp.dot(vt_ref[g, :, pl.ds(starts[a], t)], w.astype(BF16),
                               preferred_element_type=F32))
            carry = carries[n] + jnp.sum(log_1ms[n], axis=0, keepdims=True)
            new_carries.append(jnp.where(blocks[a] <= 0, NEG, carry))
        return new_carries, pvs

    def any_alive(carries):
        top = carries[0]
        for c in carries[1:]:
            top = jnp.maximum(top, c)
        return jnp.max(top, axis=1, keepdims=True)[0, 0] > ZERO_WEIGHT_LOG2

    first = [SB_TILES * ip + a for a in range(SB_TILES)]
    carries, pvs = block(first, [jnp.zeros((1, t), F32) for _ in tasks], True)
    for n in range(len(tasks)):
        acc_ref[n] = pvs[n]

    def cond(state):
        step, alive = state[0], state[1]
        return jnp.logical_and(step < first[-1], alive)

    def body(state):
        step = state[0]
        carries, pvs = block([j - 1 - step for j in first], list(state[2:]), False)
        for n in range(len(tasks)):
            acc_ref[n] += pvs[n]
        return (step + 1, any_alive(carries), *carries)

    lax.while_loop(cond, body, (jnp.int32(0), any_alive(carries), *carries))
    for n, (a, g) in enumerate(tasks):
        o_ref[a * t:(a + 1) * t, _head_cols(g)] = acc_ref[n].T.astype(o_ref.dtype)


def sb_attention(p2, batch, seq):
    t = ATT_TILE
    rows = SB_TILES * t
    nq = seq // rows
    qc, kc, vc = N_GROUPS, 2 * N_GROUPS, 3 * N_GROUPS
    return pl.pallas_call(
        _sb_kernel,
        grid=(batch, N_GROUPS, nq),
        in_specs=[pl.BlockSpec((rows, GROUP_WIDTH), lambda b, h, i: (b * nq + i, qc + h)),
                  pl.BlockSpec((seq, GROUP_WIDTH), lambda b, h, i: (b, kc + h)),
                  pl.BlockSpec((seq, GROUP_WIDTH), lambda b, h, i: (b, vc + h))],
        out_specs=pl.BlockSpec((rows, GROUP_WIDTH), lambda b, h, i: (b * nq + i, h)),
        out_shape=jax.ShapeDtypeStruct((batch * seq, WIDTH), BF16),
        scratch_shapes=[pltpu.VMEM((HEADS_PER_STEP, HEAD_DIM, seq), BF16),
                        pltpu.VMEM((SB_TILES * HEADS_PER_STEP, HEAD_DIM, t), F32)],
        compiler_params=_params(("arbitrary", "arbitrary", "arbitrary")),
        name="sb_attention",
    )(p2, p2, p2)


def _moba_kernel(slopes_ref, inv_span_ref, q_ref, k_ref, v_ref, o_ref, vt_ref, kmh_ref, kml_ref,
                 kbias_ref, selb_ref, acc_ref, knorm_ref, *p_refs, n_blocks):
    hg = pl.program_id(1)
    i = pl.program_id(2)
    t = ATT_TILE
    heads = range(HEADS_PER_STEP)
    n_heads = HEADS_PER_STEP
    seq = k_ref.shape[0]

    row = lax.broadcasted_iota(jnp.int32, (t, t), 0)
    col = lax.broadcasted_iota(jnp.int32, (t, t), 1)
    slope_log2 = [slopes_ref[hg * HEADS_PER_STEP + g] * LOG2E for g in heads]
    inv_span = [inv_span_ref[hg * HEADS_PER_STEP + g] for g in heads]

    @pl.when(i == 0)
    def _():
        for g in heads:
            _transpose_values(v_ref, vt_ref, g)
            vt_ref[g, HEAD_DIM:, :] = jnp.ones((VT_ROWS - HEAD_DIM, seq + t), BF16)
            vt_ref[g, :HEAD_DIM, pl.ds(seq, t)] = jnp.zeros((HEAD_DIM, t), BF16)
            k32 = k_ref[:, _head_cols(g)].astype(F32)
            kn2 = jnp.max(jnp.sum(k32 * k32, axis=1, keepdims=True), axis=0, keepdims=True)
            knorm_ref[g] = jnp.broadcast_to(kn2, knorm_ref.shape[1:])
            km = jnp.mean(k32.reshape(n_blocks, t, HEAD_DIM), axis=1)
            hi = km.astype(BF16)
            kmh_ref[g] = hi
            kml_ref[g] = (km - hi.astype(F32)).astype(BF16)
            kbias_ref[g] = row.astype(F32) * slope_log2[g]

    blk = lax.broadcasted_iota(jnp.int32, (n_blocks, t), 0)
    valid = blk < i

    def scores(j, g):
        k0 = pl.multiple_of(j * t, t)
        return _nt_dot(k_ref[pl.ds(k0, t), _head_cols(g)], q_ref[:, _head_cols(g)])

    def weighted_sum(j, g, slot, alpha):
        k0 = pl.multiple_of(j * t, t)
        k1 = pl.multiple_of((j + 1) * t, t)
        acc_ref[g] = (alpha * acc_ref[g]
                      + jnp.dot(vt_ref[g, :, pl.ds(k0, t)], p_refs[g][slot, 0],
                                preferred_element_type=F32)
                      + jnp.dot(vt_ref[g, :, pl.ds(k1, t)], p_refs[g][slot, 1],
                                preferred_element_type=F32))

    ms, own_ps, keeps = [], [], []
    for g in heads:
        q = q_ref[:, _head_cols(g)]
        gate = _nt_dot(kmh_ref[g], q) + _nt_dot(kml_ref[g], q)
        gate = jnp.where(valid, gate, NEG)
        chosen = jnp.zeros((n_blocks, t), jnp.bool_)
        for _ in range(MOBA_TOPK):
            top = jnp.max(gate, axis=0, keepdims=True)
            first = jnp.min(jnp.where(gate == top, blk, n_blocks), axis=0, keepdims=True)
            pick = blk == first
            chosen = jnp.logical_or(chosen, pick)
            gate = jnp.where(pick, -jnp.inf, gate)
        selb_ref[g] = jnp.where(jnp.logical_and(chosen, valid), 0.0, NEG)

        x = jnp.where(col >= row, scores(i, g) + kbias_ref[g], NEG)
        m = jnp.max(x, axis=0, keepdims=True)
        ms.append(m)
        own_ps.append(jnp.exp2(x - m).astype(BF16))
        acc_ref[g] = jnp.zeros((VT_ROWS, t), F32)

        q32 = q.astype(F32)
        qn2 = jnp.max(jnp.sum(q32 * q32, axis=1, keepdims=True), axis=0, keepdims=True)
        reach = jnp.sqrt(qn2 * knorm_ref[g, :1, :1]) - jnp.min(m, axis=1, keepdims=True)
        keeps.append((reach - ZERO_WEIGHT_LOG2) * inv_span[g] + 2.0)

    keep = keeps[0]
    for other in keeps[1:]:
        keep = jnp.maximum(keep, other)
    keep = jnp.clip(keep, 0.0, float(n_blocks)).astype(jnp.int32)[0, 0]
    first_step = jnp.maximum(i - keep, 0) >> 1
    for g in heads:
        p_refs[g][first_step & 1, 0] = own_ps[g]
        p_refs[g][first_step & 1, 1] = jnp.zeros((t, t), BF16)

    def body(step, state):
        prev, state = state[0], state[1:]
        alphas, ms = state[:n_heads], state[n_heads:]
        slot = step & 1
        j = 2 * step
        for g in heads:
            weighted_sum(prev, g, slot, alphas[g])
        dots = [(scores(j, g), scores(j + 1, g)) for g in heads]
        new_alphas, new_ms = [], []
        for g in heads:
            x0 = dots[g][0] + kbias_ref[g]
            x1 = dots[g][1] + kbias_ref[g]
            base = slope_log2[g] * ((j - i) * t).astype(F32)
            shift0 = base + selb_ref[g, pl.ds(j, 1), :]
            shift1 = (base + slope_log2[g] * t) + selb_ref[g, pl.ds(j + 1, 1), :]
            m_new = jnp.maximum(ms[g], jnp.maximum(jnp.max(x0, axis=0, keepdims=True) + shift0,
                                                   jnp.max(x1, axis=0, keepdims=True) + shift1))
            new_alphas.append(jnp.exp2(ms[g] - m_new))
            p_refs[g][1 - slot, 0] = jnp.exp2(x0 - (m_new - shift0)).astype(BF16)
            p_refs[g][1 - slot, 1] = jnp.exp2(x1 - (m_new - shift1)).astype(BF16)
            new_ms.append(m_new)
        return (j, *new_alphas, *new_ms)

    n_steps = (i + 1) >> 1
    ones = [jnp.ones((1, t), F32) for _ in heads]
    state = lax.fori_loop(first_step, n_steps, body, (i, *ones, *ms))
    prev, alphas = state[0], state[1:1 + n_heads]
    for g in heads:
        weighted_sum(prev, g, n_steps & 1, alphas[g])
        acc = acc_ref[g]
        o_ref[:, _head_cols(g)] = (acc[:HEAD_DIM] / acc[HEAD_DIM:HEAD_DIM + 1]).T.astype(o_ref.dtype)


def moba_attention(qk, p2, slopes, batch, seq):
    t = ATT_TILE
    nq = seq // t
    n_blocks = seq // MOBA_BLOCK
    grid_spec = pltpu.PrefetchScalarGridSpec(
        num_scalar_prefetch=2,
        grid=(batch, N_GROUPS, nq),
        in_specs=[pl.BlockSpec((t, GROUP_WIDTH), lambda b, h, i, s, r: (b * nq + i, h)),
                  pl.BlockSpec((seq, GROUP_WIDTH), lambda b, h, i, s, r: (b, N_GROUPS + h)),
                  pl.BlockSpec((seq, GROUP_WIDTH), lambda b, h, i, s, r: (b, h))],
        out_specs=pl.BlockSpec((t, GROUP_WIDTH), lambda b, h, i, s, r: (b * nq + i, h)),
        scratch_shapes=[pltpu.VMEM((HEADS_PER_STEP, VT_ROWS, seq + t), BF16),
                        pltpu.VMEM((HEADS_PER_STEP, n_blocks, HEAD_DIM), BF16),
                        pltpu.VMEM((HEADS_PER_STEP, n_blocks, HEAD_DIM), BF16),
                        pltpu.VMEM((HEADS_PER_STEP, t, t), F32),
                        pltpu.VMEM((HEADS_PER_STEP, n_blocks, t), F32),
                        pltpu.VMEM((HEADS_PER_STEP, VT_ROWS, t), F32),
                        pltpu.VMEM((HEADS_PER_STEP, 8, HEAD_DIM), F32)]
                       + [pltpu.VMEM((2, 2, t, t), BF16) for _ in range(HEADS_PER_STEP)],
    )
    return pl.pallas_call(
        functools.partial(_moba_kernel, n_blocks=n_blocks),
        grid_spec=grid_spec,
        out_shape=jax.ShapeDtypeStruct((batch * seq, WIDTH), BF16),
        compiler_params=_params(("arbitrary", "arbitrary", "arbitrary")),
        name="moba_attention",
    )(slopes, 1.0 / (slopes * (LOG2E * MOBA_BLOCK)), qk, qk, p2)


def _layer(x, layer, w_in, weights, vectors, slopes, batch, seq):
    g_mix, b_gate, g_q, g_k, g_mlp = vectors
    depth = weights["w_in"].shape[0]
    h = rms_norm(x, g_mix)
    qk_gain = jnp.concatenate([jnp.tile(g_q * SCORE_LOG2, N_HEADS),
                               jnp.tile(g_k, N_HEADS)]).reshape(1, 2 * WIDTH)
    qk, w_out, w_bm, w_bs = matmul(
        _mm_headnorm_kernel, h, w_in, [(qk_gain, "row")], n=2 * WIDTH, col_off=0,
        out_dtype=BF16, tm=1024, tn=1024, name="in_proj_qk_moba",
        side_casts=[(weights["w_out"], layer), (weights["w_branch_moba"], layer),
                    (weights["w_branch_sb"], layer)])
    rest_scale = jnp.concatenate([jnp.ones((WIDTH,), F32), jnp.full((WIDTH,), SCORE_LOG2, F32),
                                  jnp.ones((2 * WIDTH,), F32)]).reshape(1, 4 * WIDTH)
    p2, w_up = matmul(_mm_colscale_kernel, h, w_in, [(rest_scale, "row")], n=4 * WIDTH,
                      col_off=2 * WIDTH, out_dtype=BF16, tm=1024, tn=1024, name="in_proj_rest",
                      side_casts=[(weights["w_up"], layer)])
    gates = matmul(_mm_sigmoid_kernel, h, w_in, [(b_gate.reshape(1, -1), "row")], n=2 * D_MODEL,
                   col_off=6 * WIDTH, out_dtype=F32, tm=1024, tn=1024, name="in_proj_gates")
    ya = moba_attention(qk, p2, slopes, batch, seq)
    yb = sb_attention(p2, batch, seq)
    merged = merge_branches(ya, yb, w_bm, w_bs, gates)
    x = matmul(_mm_residual_kernel, merged, w_out, [(x, "tile")], n=D_MODEL, col_off=0,
               out_dtype=F32, tm=1024, tn=1024, name="out_proj")
    h2 = rms_norm(x, g_mlp)
    up_casts = [(weights["w_down"], layer)]
    if layer + 1 < depth:
        up_casts.append((weights["w_in"], layer + 1))
    u, w_down, *next_w_in = matmul(_mm_relu2_kernel, h2, w_up, [], n=D_FF, col_off=0,
                                   out_dtype=BF16, tm=1024, tn=1024, name="mlp_up",
                                   side_casts=up_casts)
    x = matmul_kgrid_residual(u, w_down, x, tm=1024, tn=1024, tk=2048, name="mlp_down")
    return x, (next_w_in[0] if next_w_in else None)


def kernel(x, norm_mix, w_in, b_gate, q_norm, k_norm, w_branch_moba, w_branch_sb, w_out,
           norm_mlp, w_up, w_down):
    batch, seq, d = x.shape
    depth = w_in.shape[0]
    slopes = jnp.exp2(-8.0 * jnp.arange(1, N_HEADS + 1, dtype=F32) / N_HEADS)
    weights = dict(w_in=w_in, w_branch_moba=w_branch_moba, w_branch_sb=w_branch_sb, w_out=w_out,
                   w_up=w_up, w_down=w_down)
    y = x.reshape(batch * seq, d)
    layer_w_in = layer_weight_bf16(w_in, 0)
    for l in range(depth):
        vectors = (norm_mix[l], b_gate[l], q_norm[l], k_norm[l], norm_mlp[l])
        y, layer_w_in = _layer(y, l, layer_w_in, weights, vectors, slopes, batch, seq)
    return y.reshape(batch, seq, d)
```

```python
import functools

import jax
import jax.numpy as jnp
from jax import lax
from jax.experimental import pallas as pl
from jax.experimental.pallas import tpu as pltpu

D_MODEL = 4096
HEAD_DIM = 128
N_HEADS = 16
WIDTH = N_HEADS * HEAD_DIM
MOBA_BLOCK = 256
MOBA_TOPK = 3
D_FF = 4 * D_MODEL
RMS_EPS = 1e-6
NEG = -1e30
SCALE = HEAD_DIM ** -0.5

VMEM_LIMIT_BYTES = 56 * 1024 * 1024

F32 = jnp.float32
BF16 = jnp.bfloat16


def _params(semantics):
    return pltpu.CompilerParams(dimension_semantics=semantics, vmem_limit_bytes=VMEM_LIMIT_BYTES)


STAT_LANES = 128


def _row_sumsq(x):
    return jnp.broadcast_to(jnp.sum(x * x, axis=1, keepdims=True), (x.shape[0], STAT_LANES))


def _rms_factor(ssq_ref, width):
    r = lax.rsqrt(ssq_ref[...] * (1.0 / D_MODEL) + RMS_EPS)
    return r if width == STAT_LANES else jnp.concatenate([r] * (width // STAT_LANES), axis=1)


def _norm_prep_kernel(x_ref, g_ref, xg_ref, ssq_ref):
    x = x_ref[...]
    xg_ref[...] = (x * g_ref[...]).astype(xg_ref.dtype)
    ssq_ref[...] = _row_sumsq(x)


def norm_prep(x, g, *, tm=256):
    m, d = x.shape
    return pl.pallas_call(
        _norm_prep_kernel,
        grid=(m // tm,),
        in_specs=[pl.BlockSpec((tm, d), lambda i: (i, 0)),
                  pl.BlockSpec((1, d), lambda i: (0, 0))],
        out_specs=[pl.BlockSpec((tm, d), lambda i: (i, 0)),
                   pl.BlockSpec((tm, STAT_LANES), lambda i: (i, 0))],
        out_shape=[jax.ShapeDtypeStruct((m, d), BF16),
                   jax.ShapeDtypeStruct((m, STAT_LANES), F32)],
        compiler_params=_params(("arbitrary",)),
        name="norm_prep",
    )(x, g.reshape(1, d))


def _cast_kernel(w_ref, o_ref):
    o_ref[...] = w_ref[...].astype(o_ref.dtype)


def layer_weight_bf16(w, layer, *, tr=512, tc=4096):
    _, rows, cols = w.shape
    tc = min(tc, cols)
    return pl.pallas_call(
        _cast_kernel,
        grid=(rows // tr, cols // tc),
        in_specs=[pl.BlockSpec((None, tr, tc), lambda i, j: (layer, i, j))],
        out_specs=pl.BlockSpec((tr, tc), lambda i, j: (i, j)),
        out_shape=jax.ShapeDtypeStruct((rows, cols), BF16),
        compiler_params=_params(("arbitrary", "arbitrary")),
        name="weight_cast",
    )(w)


def _mm_colscale_kernel(a_ref, w_ref, s_ref, ssq_ref, o_ref):
    acc = jnp.dot(a_ref[...], w_ref[...], preferred_element_type=F32)
    o_ref[...] = ((acc * _rms_factor(ssq_ref, acc.shape[1])) * s_ref[...]).astype(o_ref.dtype)


def _mm_headnorm_kernel(a_ref, w_ref, g_ref, ssq_ref, o_ref):
    acc = jnp.dot(a_ref[...], w_ref[...], preferred_element_type=F32)
    r = _rms_factor(ssq_ref, HEAD_DIM)
    for c in range(acc.shape[1] // HEAD_DIM):
        sl = slice(c * HEAD_DIM, (c + 1) * HEAD_DIM)
        y = acc[:, sl] * r
        ms = jnp.mean(y * y, axis=-1, keepdims=True)
        o_ref[:, sl] = ((y * lax.rsqrt(ms + RMS_EPS)) * g_ref[:, sl]).astype(o_ref.dtype)


def _mm_sigmoid_kernel(a_ref, w_ref, b_ref, ssq_ref, o_ref):
    acc = jnp.dot(a_ref[...], w_ref[...], preferred_element_type=F32)
    o_ref[...] = jax.nn.sigmoid(acc * _rms_factor(ssq_ref, acc.shape[1])
                                + b_ref[...]).astype(o_ref.dtype)


def _mm_relu2_kernel(a_ref, w_ref, ssq_ref, o_ref):
    acc = jnp.dot(a_ref[...], w_ref[...], preferred_element_type=F32)
    y = acc * _rms_factor(ssq_ref, acc.shape[1])
    o_ref[...] = jnp.square(jnp.maximum(y, 0.0)).astype(o_ref.dtype)


def _store_residual_and_norm_inputs(x, j, g_ref, o_ref, xg_ref, ssq_ref):
    o_ref[...] = x
    xg_ref[...] = (x * g_ref[...]).astype(xg_ref.dtype)

    @pl.when(j == 0)
    def _():
        ssq_ref[...] = _row_sumsq(x)

    @pl.when(j > 0)
    def _():
        ssq_ref[...] += _row_sumsq(x)


def _mm_residual_norm_kernel(a_ref, w_ref, r_ref, g_ref, o_ref, xg_ref, ssq_ref):
    acc = jnp.dot(a_ref[...], w_ref[...], preferred_element_type=F32)
    _store_residual_and_norm_inputs(r_ref[...] + acc, pl.program_id(1), g_ref, o_ref, xg_ref, ssq_ref)


def matmul_residual_norm(a, w, r, gain, *, tm, tn, name):
    m, k = a.shape
    n = w.shape[1]
    return pl.pallas_call(
        _mm_residual_norm_kernel,
        grid=(m // tm, n // tn),
        in_specs=[pl.BlockSpec((tm, k), lambda i, j: (i, 0)),
                  pl.BlockSpec((k, tn), lambda i, j: (0, j)),
                  pl.BlockSpec((tm, tn), lambda i, j: (i, j)),
                  pl.BlockSpec((1, tn), lambda i, j: (0, j))],
        out_specs=[pl.BlockSpec((tm, tn), lambda i, j: (i, j)),
                   pl.BlockSpec((tm, tn), lambda i, j: (i, j)),
                   pl.BlockSpec((tm, STAT_LANES), lambda i, j: (i, 0))],
        out_shape=[jax.ShapeDtypeStruct((m, n), F32), jax.ShapeDtypeStruct((m, n), BF16),
                   jax.ShapeDtypeStruct((m, STAT_LANES), F32)],
        compiler_params=_params(("arbitrary", "arbitrary")),
        name=name,
    )(a, w, r, gain.reshape(1, n))


def _with_side_casts(kernel, n_in, n_side, *refs):
    ins, side_ins = refs[:n_in], refs[n_in:n_in + n_side]
    out, side_outs = refs[n_in + n_side], refs[n_in + n_side + 1:]
    kernel(*ins, out)
    for src, dst in zip(side_ins, side_outs):
        dst[...] = src[...].astype(dst.dtype)


def _side_cast_specs(side_casts, grid):
    gi, gj = grid
    in_specs, out_specs, out_shapes, arrays = [], [], [], []
    for w, layer in side_casts:
        _, rows, cols = w.shape
        br, bc = rows // gi, cols // gj
        assert br * gi == rows and bc * gj == cols and br % 16 == 0 and bc % 128 == 0
        in_specs.append(pl.BlockSpec((None, br, bc), lambda i, j, layer=layer: (layer, i, j)))
        out_specs.append(pl.BlockSpec((br, bc), lambda i, j: (i, j)))
        out_shapes.append(jax.ShapeDtypeStruct((rows, cols), BF16))
        arrays.append(w)
    return in_specs, out_specs, out_shapes, arrays


def matmul(kernel, a, w, extras, *, n, col_off, out_dtype, tm, tn, name, side_casts=()):
    m, k = a.shape
    off = col_off // tn
    assert off * tn == col_off and n % tn == 0 and m % tm == 0
    grid = (m // tm, n // tn)
    in_specs = [pl.BlockSpec((tm, k), lambda i, j: (i, 0)),
                pl.BlockSpec((k, tn), lambda i, j: (0, j + off))]
    args = [a, w]
    for arr, kind in extras:
        if kind == "row":
            in_specs.append(pl.BlockSpec((1, tn), lambda i, j: (0, j)))
        elif kind == "rowstat":
            in_specs.append(pl.BlockSpec((tm, STAT_LANES), lambda i, j: (i, 0)))
        else:
            in_specs.append(pl.BlockSpec((tm, tn), lambda i, j: (i, j)))
        args.append(arr)
    side_in, side_out, side_shapes, side_arrays = _side_cast_specs(side_casts, grid)
    outs = pl.pallas_call(
        functools.partial(_with_side_casts, kernel, len(args), len(side_arrays)),
        grid=grid,
        in_specs=in_specs + side_in,
        out_specs=[pl.BlockSpec((tm, tn), lambda i, j: (i, j))] + side_out,
        out_shape=[jax.ShapeDtypeStruct((m, n), out_dtype)] + side_shapes,
        compiler_params=_params(("arbitrary", "arbitrary")),
        name=name,
    )(*args, *side_arrays)
    return outs if side_casts else outs[0]


def _mm_kgrid_residual_kernel(a_ref, w_ref, r_ref, *rest, with_norm):
    acc_ref = rest[-1]
    kk = pl.program_id(2)

    @pl.when(kk == 0)
    def _():
        acc_ref[...] = jnp.zeros_like(acc_ref)

    acc_ref[...] += jnp.dot(a_ref[...], w_ref[...], preferred_element_type=F32)

    @pl.when(kk == pl.num_programs(2) - 1)
    def _():
        x = r_ref[...] + acc_ref[...]
        if with_norm:
            g_ref, o_ref, xg_ref, ssq_ref = rest[:4]
            _store_residual_and_norm_inputs(x, pl.program_id(1), g_ref, o_ref, xg_ref, ssq_ref)
        else:
            rest[0][...] = x


def matmul_kgrid_residual(a, w, r, gain=None, *, tm, tn, tk, name):
    m, k = a.shape
    n = w.shape[1]
    with_norm = gain is not None
    in_specs = [pl.BlockSpec((tm, tk), lambda i, j, kk: (i, kk)),
                pl.BlockSpec((tk, tn), lambda i, j, kk: (kk, j)),
                pl.BlockSpec((tm, tn), lambda i, j, kk: (i, j))]
    out_specs = [pl.BlockSpec((tm, tn), lambda i, j, kk: (i, j))]
    out_shape = [jax.ShapeDtypeStruct((m, n), F32)]
    args = [a, w, r]
    if with_norm:
        in_specs.append(pl.BlockSpec((1, tn), lambda i, j, kk: (0, j)))
        args.append(gain.reshape(1, n))
        out_specs += [pl.BlockSpec((tm, tn), lambda i, j, kk: (i, j)),
                      pl.BlockSpec((tm, STAT_LANES), lambda i, j, kk: (i, 0))]
        out_shape += [jax.ShapeDtypeStruct((m, n), BF16),
                      jax.ShapeDtypeStruct((m, STAT_LANES), F32)]
    outs = pl.pallas_call(
        functools.partial(_mm_kgrid_residual_kernel, with_norm=with_norm),
        grid=(m // tm, n // tn, k // tk),
        in_specs=in_specs,
        out_specs=out_specs,
        out_shape=out_shape,
        scratch_shapes=[pltpu.VMEM((tm, tn), F32)],
        compiler_params=_params(("arbitrary", "arbitrary", "arbitrary")),
        name=name,
    )(*args)
    return outs if with_norm else outs[0]


def _merge_kernel(ya_ref, yb_ref, wa_ref, wb_ref, ga_ref, gb_ref, o_ref):
    pa = jnp.dot(ya_ref[...], wa_ref[...], preferred_element_type=F32)
    pb = jnp.dot(yb_ref[...], wb_ref[...], preferred_element_type=F32)
    o_ref[...] = (ga_ref[...] * pa + gb_ref[...] * pb).astype(o_ref.dtype)


def merge_branches(ya, yb, wa, wb, gates, *, tm=1024, tn=512, side_casts=()):
    m, k = ya.shape
    n = wa.shape[1]
    goff = n // tn
    grid = (m // tm, n // tn)
    side_in, side_out, side_shapes, side_arrays = _side_cast_specs(side_casts, grid)
    outs = pl.pallas_call(
        functools.partial(_with_side_casts, _merge_kernel, 6, len(side_arrays)),
        grid=grid,
        in_specs=[pl.BlockSpec((tm, k), lambda i, j: (i, 0)),
                  pl.BlockSpec((tm, k), lambda i, j: (i, 0)),
                  pl.BlockSpec((k, tn), lambda i, j: (0, j)),
                  pl.BlockSpec((k, tn), lambda i, j: (0, j)),
                  pl.BlockSpec((tm, tn), lambda i, j: (i, j)),
                  pl.BlockSpec((tm, tn), lambda i, j: (i, j + goff))] + side_in,
        out_specs=[pl.BlockSpec((tm, tn), lambda i, j: (i, j))] + side_out,
        out_shape=[jax.ShapeDtypeStruct((m, n), BF16)] + side_shapes,
        compiler_params=_params(("arbitrary", "arbitrary")),
        name="merge_branches",
    )(ya, yb, wa, wb, gates, gates, *side_arrays)
    return outs if side_casts else outs[0]


ATT_TILE = 256
HEADS_PER_STEP = 4
GROUP_WIDTH = HEADS_PER_STEP * HEAD_DIM
N_GROUPS = N_HEADS // HEADS_PER_STEP

LOG2E = 1.4426950408889634
INV_LN2 = LOG2E
SCORE_LOG2 = SCALE * LOG2E
ZERO_WEIGHT_LOG2 = -160.0
VT_ROWS = HEAD_DIM + 16


def _head_cols(g):
    return slice(g * HEAD_DIM, (g + 1) * HEAD_DIM)


def _transpose_values(v_ref, vt_ref, g):
    n_chunks = v_ref.shape[0] // ATT_TILE

    def body(c, carry):
        r0 = pl.multiple_of(c * ATT_TILE, ATT_TILE)
        vt_ref[g, :HEAD_DIM, pl.ds(r0, ATT_TILE)] = (
            v_ref[pl.ds(r0, ATT_TILE), _head_cols(g)].astype(F32).T.astype(BF16))
        return carry

    lax.fori_loop(0, n_chunks, body, 0, unroll=8)


def _nt_dot(a, b):
    return lax.dot_general(a, b, (((1,), (1,)), ((), ())), preferred_element_type=F32)


def _neg_abs(x):
    bits = lax.bitcast_convert_type(x, jnp.uint32) | jnp.uint32(0x80000000)
    return lax.bitcast_convert_type(bits, F32)


SB_TILES = 2


def _sb_kernel(q_ref, k_ref, v_ref, o_ref, vt_ref, acc_ref):
    ip = pl.program_id(2)
    t = ATT_TILE
    tasks = [(a, g) for a in range(SB_TILES) for g in range(HEADS_PER_STEP)]

    @pl.when(ip == 0)
    def _():
        for g in range(HEADS_PER_STEP):
            _transpose_values(v_ref, vt_ref, g)

    row = lax.broadcasted_iota(jnp.int32, (t, t), 0)
    col = lax.broadcasted_iota(jnp.int32, (t, t), 1)
    upper = (col > row).astype(BF16)

    def block(blocks, carries, diagonal):
        starts = [pl.multiple_of(jnp.maximum(j, 0) * t, t) for j in blocks]
        zs = [_nt_dot(k_ref[pl.ds(starts[a], t), _head_cols(g)],
                      q_ref[a * t:(a + 1) * t, _head_cols(g)]) for a, g in tasks]
        log_betas, log_1ms, cums = [], [], []
        for n in range(len(tasks)):
            z = zs[n]
            log_beta = jnp.minimum(z, 0.0) - jnp.log2(1.0 + jnp.exp2(_neg_abs(z)))
            log_1m = log_beta - z
            if diagonal:
                log_1m = jnp.where(row < col, log_1m, 0.0)
            hi = log_1m.astype(BF16)
            lo = (log_1m - hi.astype(F32)).astype(BF16)
            cums.append(jnp.dot(upper, hi, preferred_element_type=F32)
                        + jnp.dot(upper, lo, preferred_element_type=F32))
            log_betas.append(log_beta)
            log_1ms.append(log_1m)
        pvs, new_carries = [], []
        for n, (a, g) in enumerate(tasks):
            w = jnp.exp2(log_betas[n] + (cums[n] + carries[n]))
            if diagonal:
                w = jnp.where(row < col, w, 0.0)
            pvs.append(jnp.dot(vt_ref[g, :, pl.ds(starts[a], t)], w.astype(BF16),
                               preferred_element_type=F32))
            carry = carries[n] + jnp.sum(log_1ms[n], axis=0, keepdims=True)
            new_carries.append(jnp.where(blocks[a] <= 0, NEG, carry))
        return new_carries, pvs

    def any_alive(carries):
        top = carries[0]
        for c in carries[1:]:
            top = jnp.maximum(top, c)
        return jnp.max(top, axis=1, keepdims=True)[0, 0] > ZERO_WEIGHT_LOG2

    first = [SB_TILES * ip + a for a in range(SB_TILES)]
    carries, pvs = block(first, [jnp.zeros((1, t), F32) for _ in tasks], True)
    for n in range(len(tasks)):
        acc_ref[n] = pvs[n]

    def cond(state):
        step, alive = state[0], state[1]
        return jnp.logical_and(step < first[-1], alive)

    def body(state):
        step = state[0]
        carries, pvs = block([j - 1 - step for j in first], list(state[2:]), False)
        for n in range(len(tasks)):
            acc_ref[n] += pvs[n]
        return (step + 1, any_alive(carries), *carries)

    lax.while_loop(cond, body, (jnp.int32(0), any_alive(carries), *carries))
    for n, (a, g) in enumerate(tasks):
        o_ref[a * t:(a + 1) * t, _head_cols(g)] = acc_ref[n].T.astype(o_ref.dtype)


def sb_attention(p2, batch, seq):
    t = ATT_TILE
    rows = SB_TILES * t
    nq = seq // rows
    qc, kc, vc = N_GROUPS, 2 * N_GROUPS, 3 * N_GROUPS
    return pl.pallas_call(
        _sb_kernel,
        grid=(batch, N_GROUPS, nq),
        in_specs=[pl.BlockSpec((rows, GROUP_WIDTH), lambda b, h, i: (b * nq + i, qc + h)),
                  pl.BlockSpec((seq, GROUP_WIDTH), lambda b, h, i: (b, kc + h)),
                  pl.BlockSpec((seq, GROUP_WIDTH), lambda b, h, i: (b, vc + h))],
        out_specs=pl.BlockSpec((rows, GROUP_WIDTH), lambda b, h, i: (b * nq + i, h)),
        out_shape=jax.ShapeDtypeStruct((batch * seq, WIDTH), BF16),
        scratch_shapes=[pltpu.VMEM((HEADS_PER_STEP, HEAD_DIM, seq), BF16),
                        pltpu.VMEM((SB_TILES * HEADS_PER_STEP, HEAD_DIM, t), F32)],
        compiler_params=_params(("arbitrary", "arbitrary", "arbitrary")),
        name="sb_attention",
    )(p2, p2, p2)


def _moba_kernel(slopes_ref, inv_span_ref, q_ref, k_ref, v_ref, o_ref, vt_ref, kmh_ref, kml_ref,
                 kbias_ref, selb_ref, acc_ref, knorm_ref, *p_refs, n_blocks):
    hg = pl.program_id(1)
    i = pl.program_id(2)
    t = ATT_TILE
    heads = range(HEADS_PER_STEP)
    n_heads = HEADS_PER_STEP
    seq = k_ref.shape[0]

    row = lax.broadcasted_iota(jnp.int32, (t, t), 0)
    col = lax.broadcasted_iota(jnp.int32, (t, t), 1)
    slope_log2 = [slopes_ref[hg * HEADS_PER_STEP + g] * LOG2E for g in heads]
    inv_span = [inv_span_ref[hg * HEADS_PER_STEP + g] for g in heads]

    @pl.when(i == 0)
    def _():
        for g in heads:
            _transpose_values(v_ref, vt_ref, g)
            vt_ref[g, HEAD_DIM:, :] = jnp.ones((VT_ROWS - HEAD_DIM, seq + t), BF16)
            vt_ref[g, :HEAD_DIM, pl.ds(seq, t)] = jnp.zeros((HEAD_DIM, t), BF16)
            k32 = k_ref[:, _head_cols(g)].astype(F32)
            kn2 = jnp.max(jnp.sum(k32 * k32, axis=1, keepdims=True), axis=0, keepdims=True)
            knorm_ref[g] = jnp.broadcast_to(kn2, knorm_ref.shape[1:])
            km = jnp.mean(k32.reshape(n_blocks, t, HEAD_DIM), axis=1)
            hi = km.astype(BF16)
            kmh_ref[g] = hi
            kml_ref[g] = (km - hi.astype(F32)).astype(BF16)
            kbias_ref[g] = row.astype(F32) * slope_log2[g]

    blk = lax.broadcasted_iota(jnp.int32, (n_blocks, t), 0)
    valid = blk < i

    def scores(j, g):
        k0 = pl.multiple_of(j * t, t)
        return _nt_dot(k_ref[pl.ds(k0, t), _head_cols(g)], q_ref[:, _head_cols(g)])

    def weighted_sum(j, g, slot, alpha):
        k0 = pl.multiple_of(j * t, t)
        k1 = pl.multiple_of((j + 1) * t, t)
        acc_ref[g] = (alpha * acc_ref[g]
                      + jnp.dot(vt_ref[g, :, pl.ds(k0, t)], p_refs[g][slot, 0],
                                preferred_element_type=F32)
                      + jnp.dot(vt_ref[g, :, pl.ds(k1, t)], p_refs[g][slot, 1],
                                preferred_element_type=F32))

    ms, own_ps, keeps = [], [], []
    for g in heads:
        q = q_ref[:, _head_cols(g)]
        gate = _nt_dot(kmh_ref[g], q) + _nt_dot(kml_ref[g], q)
        gate = jnp.where(valid, gate, NEG)
        chosen = jnp.zeros((n_blocks, t), jnp.bool_)
        for _ in range(MOBA_TOPK):
            top = jnp.max(gate, axis=0, keepdims=True)
            first = jnp.min(jnp.where(gate == top, blk, n_blocks), axis=0, keepdims=True)
            pick = blk == first
            chosen = jnp.logical_or(chosen, pick)
            gate = jnp.where(pick, -jnp.inf, gate)
        selb_ref[g] = jnp.where(jnp.logical_and(chosen, valid), 0.0, NEG)

        x = jnp.where(col >= row, scores(i, g) + kbias_ref[g], NEG)
        m = jnp.max(x, axis=0, keepdims=True)
        ms.append(m)
        own_ps.append(jnp.exp2(x - m).astype(BF16))
        acc_ref[g] = jnp.zeros((VT_ROWS, t), F32)

        q32 = q.astype(F32)
        qn2 = jnp.max(jnp.sum(q32 * q32, axis=1, keepdims=True), axis=0, keepdims=True)
        reach = jnp.sqrt(qn2 * knorm_ref[g, :1, :1]) - jnp.min(m, axis=1, keepdims=True)
        keeps.append((reach - ZERO_WEIGHT_LOG2) * inv_span[g] + 2.0)

    keep = keeps[0]
    for other in keeps[1:]:
        keep = jnp.maximum(keep, other)
    keep = jnp.clip(keep, 0.0, float(n_blocks)).astype(jnp.int32)[0, 0]
    first_step = jnp.maximum(i - keep, 0) >> 1
    for g in heads:
        p_refs[g][first_step & 1, 0] = own_ps[g]
        p_refs[g][first_step & 1, 1] = jnp.zeros((t, t), BF16)

    def body(step, state):
        prev, state = state[0], state[1:]
        alphas, ms = state[:n_heads], state[n_heads:]
        slot = step & 1
        j = 2 * step
        for g in heads:
            weighted_sum(prev, g, slot, alphas[g])
        dots = [(scores(j, g), scores(j + 1, g)) for g in heads]
        new_alphas, new_ms = [], []
        for g in heads:
            x0 = dots[g][0] + kbias_ref[g]
            x1 = dots[g][1] + kbias_ref[g]
            base = slope_log2[g] * ((j - i) * t).astype(F32)
            shift0 = base + selb_ref[g, pl.ds(j, 1), :]
            shift1 = (base + slope_log2[g] * t) + selb_ref[g, pl.ds(j + 1, 1), :]
            m_new = jnp.maximum(ms[g], jnp.maximum(jnp.max(x0, axis=0, keepdims=True) + shift0,
                                                   jnp.max(x1, axis=0, keepdims=True) + shift1))
            new_alphas.append(jnp.exp2(ms[g] - m_new))
            p_refs[g][1 - slot, 0] = jnp.exp2(x0 - (m_new - shift0)).astype(BF16)
            p_refs[g][1 - slot, 1] = jnp.exp2(x1 - (m_new - shift1)).astype(BF16)
            new_ms.append(m_new)
        return (j, *new_alphas, *new_ms)

    n_steps = (i + 1) >> 1
    ones = [jnp.ones((1, t), F32) for _ in heads]
    state = lax.fori_loop(first_step, n_steps, body, (i, *ones, *ms))
    prev, alphas = state[0], state[1:1 + n_heads]
    for g in heads:
        weighted_sum(prev, g, n_steps & 1, alphas[g])
        acc = acc_ref[g]
        o_ref[:, _head_cols(g)] = (acc[:HEAD_DIM] / acc[HEAD_DIM:HEAD_DIM + 1]).T.astype(o_ref.dtype)


def moba_attention(qk, p2, slopes, batch, seq):
    t = ATT_TILE
    nq = seq // t
    n_blocks = seq // MOBA_BLOCK
    grid_spec = pltpu.PrefetchScalarGridSpec(
        num_scalar_prefetch=2,
        grid=(batch, N_GROUPS, nq),
        in_specs=[pl.BlockSpec((t, GROUP_WIDTH), lambda b, h, i, s, r: (b * nq + i, h)),
                  pl.BlockSpec((seq, GROUP_WIDTH), lambda b, h, i, s, r: (b, N_GROUPS + h)),
                  pl.BlockSpec((seq, GROUP_WIDTH), lambda b, h, i, s, r: (b, h))],
        out_specs=pl.BlockSpec((t, GROUP_WIDTH), lambda b, h, i, s, r: (b * nq + i, h)),
        scratch_shapes=[pltpu.VMEM((HEADS_PER_STEP, VT_ROWS, seq + t), BF16),
                        pltpu.VMEM((HEADS_PER_STEP, n_blocks, HEAD_DIM), BF16),
                        pltpu.VMEM((HEADS_PER_STEP, n_blocks, HEAD_DIM), BF16),
                        pltpu.VMEM((HEADS_PER_STEP, t, t), F32),
                        pltpu.VMEM((HEADS_PER_STEP, n_blocks, t), F32),
                        pltpu.VMEM((HEADS_PER_STEP, VT_ROWS, t), F32),
                        pltpu.VMEM((HEADS_PER_STEP, 8, HEAD_DIM), F32)]
                       + [pltpu.VMEM((2, 2, t, t), BF16) for _ in range(HEADS_PER_STEP)],
    )
    return pl.pallas_call(
        functools.partial(_moba_kernel, n_blocks=n_blocks),
        grid_spec=grid_spec,
        out_shape=jax.ShapeDtypeStruct((batch * seq, WIDTH), BF16),
        compiler_params=_params(("arbitrary", "arbitrary", "arbitrary")),
        name="moba_attention",
    )(slopes, 1.0 / (slopes * (LOG2E * MOBA_BLOCK)), qk, qk, p2)


def _layer(xg, ssq, x, layer, w_in, weights, vectors, slopes, batch, seq):
    b_gate, g_q, g_k, g_mlp, next_g_mix = vectors
    depth = weights["w_in"].shape[0]
    qk_gain = jnp.concatenate([jnp.tile(g_q * SCORE_LOG2, N_HEADS),
                               jnp.tile(g_k, N_HEADS)]).reshape(1, 2 * WIDTH)
    qk, w_out, w_bm, w_bs = matmul(
        _mm_headnorm_kernel, xg, w_in, [(qk_gain, "row"), (ssq, "rowstat")], n=2 * WIDTH,
        col_off=0, out_dtype=BF16, tm=1024, tn=1024, name="in_proj_qk_moba",
        side_casts=[(weights["w_out"], layer), (weights["w_branch_moba"], layer),
                    (weights["w_branch_sb"], layer)])
    rest_scale = jnp.concatenate([jnp.ones((WIDTH,), F32), jnp.full((WIDTH,), SCORE_LOG2, F32),
                                  jnp.ones((2 * WIDTH,), F32)]).reshape(1, 4 * WIDTH)
    p2, w_up = matmul(_mm_colscale_kernel, xg, w_in, [(rest_scale, "row"), (ssq, "rowstat")],
                      n=4 * WIDTH, col_off=2 * WIDTH, out_dtype=BF16, tm=1024, tn=1024,
                      name="in_proj_rest", side_casts=[(weights["w_up"], layer)])
    gates = matmul(_mm_sigmoid_kernel, xg, w_in, [(b_gate.reshape(1, -1), "row"), (ssq, "rowstat")],
                   n=2 * D_MODEL, col_off=6 * WIDTH, out_dtype=F32, tm=1024, tn=1024,
                   name="in_proj_gates")
    ya = moba_attention(qk, p2, slopes, batch, seq)
    yb = sb_attention(p2, batch, seq)
    merged = merge_branches(ya, yb, w_bm, w_bs, gates)
    x, xg2, ssq2 = matmul_residual_norm(merged, w_out, x, g_mlp, tm=1024, tn=512, name="out_proj")
    up_casts = [(weights["w_down"], layer)]
    if layer + 1 < depth:
        up_casts.append((weights["w_in"], layer + 1))
    u, w_down, *next_w_in = matmul(_mm_relu2_kernel, xg2, w_up, [(ssq2, "rowstat")], n=D_FF,
                                   col_off=0, out_dtype=BF16, tm=1024, tn=1024, name="mlp_up",
                                   side_casts=up_casts)
    if next_g_mix is None:
        x = matmul_kgrid_residual(u, w_down, x, tm=1024, tn=1024, tk=2048, name="mlp_down")
        return x, None, None, None
    x, xg, ssq = matmul_kgrid_residual(u, w_down, x, next_g_mix, tm=1024, tn=1024, tk=2048,
                                       name="mlp_down")
    return x, xg, ssq, next_w_in[0]


def kernel(x, norm_mix, w_in, b_gate, q_norm, k_norm, w_branch_moba, w_branch_sb, w_out,
           norm_mlp, w_up, w_down):
    batch, seq, d = x.shape
    depth = w_in.shape[0]
    slopes = jnp.exp2(-8.0 * jnp.arange(1, N_HEADS + 1, dtype=F32) / N_HEADS)
    weights = dict(w_in=w_in, w_branch_moba=w_branch_moba, w_branch_sb=w_branch_sb, w_out=w_out,
                   w_up=w_up, w_down=w_down)
    y = x.reshape(batch * seq, d)
    xg, ssq = norm_prep(y, norm_mix[0])
    layer_w_in = layer_weight_bf16(w_in, 0)
    for l in range(depth):
        next_g_mix = norm_mix[l + 1] if l + 1 < depth else None
        vectors = (b_gate[l], q_norm[l], k_norm[l], norm_mlp[l], next_g_mix)
        y, xg, ssq, layer_w_in = _layer(xg, ssq, y, l, layer_w_in, weights, vectors, slopes,
                                        batch, seq)
    return y.reshape(batch, seq, d)
```

```python
import functools

import jax
import jax.numpy as jnp
from jax import lax
from jax.experimental import pallas as pl
from jax.experimental.pallas import tpu as pltpu

D_MODEL = 4096
HEAD_DIM = 128
N_HEADS = 16
WIDTH = N_HEADS * HEAD_DIM
MOBA_BLOCK = 256
MOBA_TOPK = 3
D_FF = 4 * D_MODEL
RMS_EPS = 1e-6
NEG = -1e30
SCALE = HEAD_DIM ** -0.5

VMEM_LIMIT_BYTES = 56 * 1024 * 1024

F32 = jnp.float32
BF16 = jnp.bfloat16


def _params(semantics):
    return pltpu.CompilerParams(dimension_semantics=semantics, vmem_limit_bytes=VMEM_LIMIT_BYTES)


def _rms_kernel(x_ref, g_ref, o_ref):
    x = x_ref[...]
    ms = jnp.mean(x * x, axis=-1, keepdims=True)
    o_ref[...] = ((x * lax.rsqrt(ms + RMS_EPS)) * g_ref[...]).astype(o_ref.dtype)


def rms_norm(x, g, *, tm=256):
    m, d = x.shape
    return pl.pallas_call(
        _rms_kernel,
        grid=(m // tm,),
        in_specs=[pl.BlockSpec((tm, d), lambda i: (i, 0)),
                  pl.BlockSpec((1, d), lambda i: (0, 0))],
        out_specs=pl.BlockSpec((tm, d), lambda i: (i, 0)),
        out_shape=jax.ShapeDtypeStruct((m, d), BF16),
        compiler_params=_params(("arbitrary",)),
        name="rms_norm",
    )(x, g.reshape(1, d))


def _cast_kernel(w_ref, o_ref):
    o_ref[...] = w_ref[...].astype(o_ref.dtype)


def layer_weight_bf16(w, layer, *, tr=512, tc=4096):
    _, rows, cols = w.shape
    tc = min(tc, cols)
    return pl.pallas_call(
        _cast_kernel,
        grid=(rows // tr, cols // tc),
        in_specs=[pl.BlockSpec((None, tr, tc), lambda i, j: (layer, i, j))],
        out_specs=pl.BlockSpec((tr, tc), lambda i, j: (i, j)),
        out_shape=jax.ShapeDtypeStruct((rows, cols), BF16),
        compiler_params=_params(("arbitrary", "arbitrary")),
        name="weight_cast",
    )(w)


def _mm_colscale_kernel(a_ref, w_ref, s_ref, o_ref):
    acc = jnp.dot(a_ref[...], w_ref[...], preferred_element_type=F32)
    o_ref[...] = (acc * s_ref[...]).astype(o_ref.dtype)


def _mm_headnorm_kernel(a_ref, w_ref, g_ref, o_ref):
    acc = jnp.dot(a_ref[...], w_ref[...], preferred_element_type=F32)
    for c in range(acc.shape[1] // HEAD_DIM):
        sl = slice(c * HEAD_DIM, (c + 1) * HEAD_DIM)
        y = acc[:, sl]
        ms = jnp.mean(y * y, axis=-1, keepdims=True)
        o_ref[:, sl] = ((y * lax.rsqrt(ms + RMS_EPS)) * g_ref[:, sl]).astype(o_ref.dtype)


def _mm_sigmoid_kernel(a_ref, w_ref, b_ref, o_ref):
    acc = jnp.dot(a_ref[...], w_ref[...], preferred_element_type=F32)
    o_ref[...] = (0.5 * jnp.tanh(0.5 * (acc + b_ref[...])) + 0.5).astype(o_ref.dtype)


def _mm_relu2_kernel(a_ref, w_ref, o_ref):
    acc = jnp.dot(a_ref[...], w_ref[...], preferred_element_type=F32)
    o_ref[...] = jnp.square(jnp.maximum(acc, 0.0)).astype(o_ref.dtype)


def _mm_residual_kernel(a_ref, w_ref, r_ref, o_ref):
    acc = jnp.dot(a_ref[...], w_ref[...], preferred_element_type=F32)
    o_ref[...] = r_ref[...] + acc


def _with_side_casts(kernel, n_in, n_side, *refs):
    ins, side_ins = refs[:n_in], refs[n_in:n_in + n_side]
    out, side_outs = refs[n_in + n_side], refs[n_in + n_side + 1:]
    kernel(*ins, out)
    for src, dst in zip(side_ins, side_outs):
        dst[...] = src[...].astype(dst.dtype)


def _side_cast_specs(side_casts, grid):
    gi, gj = grid
    in_specs, out_specs, out_shapes, arrays = [], [], [], []
    for w, layer in side_casts:
        _, rows, cols = w.shape
        br, bc = rows // gi, cols // gj
        assert br * gi == rows and bc * gj == cols and br % 16 == 0 and bc % 128 == 0
        in_specs.append(pl.BlockSpec((None, br, bc), lambda i, j, layer=layer: (layer, i, j)))
        out_specs.append(pl.BlockSpec((br, bc), lambda i, j: (i, j)))
        out_shapes.append(jax.ShapeDtypeStruct((rows, cols), BF16))
        arrays.append(w)
    return in_specs, out_specs, out_shapes, arrays


def matmul(kernel, a, w, extras, *, n, col_off, out_dtype, tm, tn, name, side_casts=()):
    m, k = a.shape
    off = col_off // tn
    assert off * tn == col_off and n % tn == 0 and m % tm == 0
    grid = (m // tm, n // tn)
    in_specs = [pl.BlockSpec((tm, k), lambda i, j: (i, 0)),
                pl.BlockSpec((k, tn), lambda i, j: (0, j + off))]
    args = [a, w]
    for arr, kind in extras:
        if kind == "row":
            in_specs.append(pl.BlockSpec((1, tn), lambda i, j: (0, j)))
        else:
            in_specs.append(pl.BlockSpec((tm, tn), lambda i, j: (i, j)))
        args.append(arr)
    side_in, side_out, side_shapes, side_arrays = _side_cast_specs(side_casts, grid)
    outs = pl.pallas_call(
        functools.partial(_with_side_casts, kernel, len(args), len(side_arrays)),
        grid=grid,
        in_specs=in_specs + side_in,
        out_specs=[pl.BlockSpec((tm, tn), lambda i, j: (i, j))] + side_out,
        out_shape=[jax.ShapeDtypeStruct((m, n), out_dtype)] + side_shapes,
        compiler_params=_params(("arbitrary", "arbitrary")),
        name=name,
    )(*args, *side_arrays)
    return outs if side_casts else outs[0]


def _mm_kgrid_residual_kernel(a_ref, w_ref, r_ref, o_ref, acc_ref):
    kk = pl.program_id(2)

    @pl.when(kk == 0)
    def _():
        acc_ref[...] = jnp.zeros_like(acc_ref)

    acc_ref[...] += jnp.dot(a_ref[...], w_ref[...], preferred_element_type=F32)

    @pl.when(kk == pl.num_programs(2) - 1)
    def _():
        o_ref[...] = r_ref[...] + acc_ref[...]


def matmul_kgrid_residual(a, w, r, *, tm, tn, tk, name):
    m, k = a.shape
    n = w.shape[1]
    return pl.pallas_call(
        _mm_kgrid_residual_kernel,
        grid=(m // tm, n // tn, k // tk),
        in_specs=[pl.BlockSpec((tm, tk), lambda i, j, kk: (i, kk)),
                  pl.BlockSpec((tk, tn), lambda i, j, kk: (kk, j)),
                  pl.BlockSpec((tm, tn), lambda i, j, kk: (i, j))],
        out_specs=pl.BlockSpec((tm, tn), lambda i, j, kk: (i, j)),
        out_shape=jax.ShapeDtypeStruct((m, n), F32),
        scratch_shapes=[pltpu.VMEM((tm, tn), F32)],
        compiler_params=_params(("arbitrary", "arbitrary", "arbitrary")),
        name=name,
    )(a, w, r)


def _merge_kernel(ya_ref, yb_ref, wa_ref, wb_ref, ga_ref, gb_ref, o_ref):
    pa = jnp.dot(ya_ref[...], wa_ref[...], preferred_element_type=F32)
    pb = jnp.dot(yb_ref[...], wb_ref[...], preferred_element_type=F32)
    o_ref[...] = (ga_ref[...] * pa + gb_ref[...] * pb).astype(o_ref.dtype)


def merge_branches(ya, yb, wa, wb, gates, *, tm=1024, tn=512, side_casts=()):
    m, k = ya.shape
    n = wa.shape[1]
    goff = n // tn
    grid = (m // tm, n // tn)
    side_in, side_out, side_shapes, side_arrays = _side_cast_specs(side_casts, grid)
    outs = pl.pallas_call(
        functools.partial(_with_side_casts, _merge_kernel, 6, len(side_arrays)),
        grid=grid,
        in_specs=[pl.BlockSpec((tm, k), lambda i, j: (i, 0)),
                  pl.BlockSpec((tm, k), lambda i, j: (i, 0)),
                  pl.BlockSpec((k, tn), lambda i, j: (0, j)),
                  pl.BlockSpec((k, tn), lambda i, j: (0, j)),
                  pl.BlockSpec((tm, tn), lambda i, j: (i, j)),
                  pl.BlockSpec((tm, tn), lambda i, j: (i, j + goff))] + side_in,
        out_specs=[pl.BlockSpec((tm, tn), lambda i, j: (i, j))] + side_out,
        out_shape=[jax.ShapeDtypeStruct((m, n), BF16)] + side_shapes,
        compiler_params=_params(("arbitrary", "arbitrary")),
        name="merge_branches",
    )(ya, yb, wa, wb, gates, gates, *side_arrays)
    return outs if side_casts else outs[0]


ATT_TILE = 256
HEADS_PER_STEP = 4
GROUP_WIDTH = HEADS_PER_STEP * HEAD_DIM
N_GROUPS = N_HEADS // HEADS_PER_STEP

LOG2E = 1.4426950408889634
SCORE_LOG2 = SCALE * LOG2E
ZERO_WEIGHT_LOG2 = -160.0
VT_ROWS = HEAD_DIM + 16
MOBA_STEP_BLOCKS = 2


def _head_cols(g):
    return slice(g * HEAD_DIM, (g + 1) * HEAD_DIM)


def _transpose_values(v_ref, vt_ref, g):
    n_chunks = v_ref.shape[0] // ATT_TILE

    def body(c, carry):
        r0 = pl.multiple_of(c * ATT_TILE, ATT_TILE)
        vt_ref[g, :HEAD_DIM, pl.ds(r0, ATT_TILE)] = (
            v_ref[pl.ds(r0, ATT_TILE), _head_cols(g)].astype(F32).T.astype(BF16))
        return carry

    lax.fori_loop(0, n_chunks, body, 0, unroll=8)


def _nt_dot(a, b):
    return lax.dot_general(a, b, (((1,), (1,)), ((), ())), preferred_element_type=F32)


def _neg_abs(x):
    bits = lax.bitcast_convert_type(x, jnp.uint32) | jnp.uint32(0x80000000)
    return lax.bitcast_convert_type(bits, F32)


SB_TILES = 2


def _sb_kernel(q_ref, k_ref, v_ref, o_ref, vt_ref, acc_ref):
    ip = pl.program_id(2)
    t = ATT_TILE
    tasks = [(a, g) for a in range(SB_TILES) for g in range(HEADS_PER_STEP)]

    @pl.when(ip == 0)
    def _():
        for g in range(HEADS_PER_STEP):
            _transpose_values(v_ref, vt_ref, g)

    row = lax.broadcasted_iota(jnp.int32, (t, t), 0)
    col = lax.broadcasted_iota(jnp.int32, (t, t), 1)
    upper = (col > row).astype(BF16)

    def block(blocks, carries, diagonal):
        starts = [pl.multiple_of(jnp.maximum(j, 0) * t, t) for j in blocks]
        zs = [_nt_dot(k_ref[pl.ds(starts[a], t), _head_cols(g)],
                      q_ref[a * t:(a + 1) * t, _head_cols(g)]) for a, g in tasks]
        log_betas, log_1ms, cums = [], [], []
        for n in range(len(tasks)):
            z = zs[n]
            log_beta = jnp.minimum(z, 0.0) - jnp.log2(1.0 + jnp.exp2(_neg_abs(z)))
            log_1m = log_beta - z
            if diagonal:
                log_1m = jnp.where(row < col, log_1m, 0.0)
            hi = log_1m.astype(BF16)
            lo = (log_1m - hi.astype(F32)).astype(BF16)
            cums.append(jnp.dot(upper, hi, preferred_element_type=F32)
                        + jnp.dot(upper, lo, preferred_element_type=F32))
            log_betas.append(log_beta)
            log_1ms.append(log_1m)
        pvs, new_carries = [], []
        for n, (a, g) in enumerate(tasks):
            w = jnp.exp2(log_betas[n] + (cums[n] + carries[n]))
            if diagonal:
                w = jnp.where(row < col, w, 0.0)
            pvs.append(jnp.dot(vt_ref[g, :, pl.ds(starts[a], t)], w.astype(BF16),
                               preferred_element_type=F32))
            carry = carries[n] + jnp.sum(log_1ms[n], axis=0, keepdims=True)
            new_carries.append(jnp.where(blocks[a] <= 0, NEG, carry))
        return new_carries, pvs

    def any_alive(carries):
        top = carries[0]
        for c in carries[1:]:
            top = jnp.maximum(top, c)
        return jnp.max(top, axis=1, keepdims=True)[0, 0] > ZERO_WEIGHT_LOG2

    first = [SB_TILES * ip + a for a in range(SB_TILES)]
    carries, pvs = block(first, [jnp.zeros((1, t), F32) for _ in tasks], True)
    for n in range(len(tasks)):
        acc_ref[n] = pvs[n]

    def cond(state):
        step, alive = state[0], state[1]
        return jnp.logical_and(step < first[-1], alive)

    def body(state):
        step = state[0]
        carries, pvs = block([j - 1 - step for j in first], list(state[2:]), False)
        for n in range(len(tasks)):
            acc_ref[n] += pvs[n]
        return (step + 1, any_alive(carries), *carries)

    lax.while_loop(cond, body, (jnp.int32(0), any_alive(carries), *carries))
    for n, (a, g) in enumerate(tasks):
        o_ref[a * t:(a + 1) * t, _head_cols(g)] = acc_ref[n].T.astype(o_ref.dtype)


def sb_attention(p2, batch, seq):
    t = ATT_TILE
    rows = SB_TILES * t
    nq = seq // rows
    qc, kc, vc = N_GROUPS, 2 * N_GROUPS, 3 * N_GROUPS
    return pl.pallas_call(
        _sb_kernel,
        grid=(batch, N_GROUPS, nq),
        in_specs=[pl.BlockSpec((rows, GROUP_WIDTH), lambda b, h, i: (b * nq + i, qc + h)),
                  pl.BlockSpec((seq, GROUP_WIDTH), lambda b, h, i: (b, kc + h)),
                  pl.BlockSpec((seq, GROUP_WIDTH), lambda b, h, i: (b, vc + h))],
        out_specs=pl.BlockSpec((rows, GROUP_WIDTH), lambda b, h, i: (b * nq + i, h)),
        out_shape=jax.ShapeDtypeStruct((batch * seq, WIDTH), BF16),
        scratch_shapes=[pltpu.VMEM((HEADS_PER_STEP, HEAD_DIM, seq), BF16),
                        pltpu.VMEM((SB_TILES * HEADS_PER_STEP, HEAD_DIM, t), F32)],
        compiler_params=_params(("arbitrary", "arbitrary", "arbitrary")),
        name="sb_attention",
    )(p2, p2, p2)


def _moba_kernel(slopes_ref, inv_span_ref, q_ref, k_ref, v_ref, o_ref, vt_ref, kmh_ref, kml_ref,
                 kbias_ref, selb_ref, acc_ref, knorm_ref, *p_refs, n_blocks):
    hg = pl.program_id(1)
    i = pl.program_id(2)
    t = ATT_TILE
    heads = range(HEADS_PER_STEP)
    n_heads = HEADS_PER_STEP
    nb = MOBA_STEP_BLOCKS
    seq = k_ref.shape[0]

    row = lax.broadcasted_iota(jnp.int32, (t, t), 0)
    col = lax.broadcasted_iota(jnp.int32, (t, t), 1)
    slope_log2 = [slopes_ref[hg * HEADS_PER_STEP + g] * LOG2E for g in heads]
    inv_span = [inv_span_ref[hg * HEADS_PER_STEP + g] for g in heads]

    @pl.when(i == 0)
    def _():
        for g in heads:
            _transpose_values(v_ref, vt_ref, g)
            vt_ref[g, HEAD_DIM:, :] = jnp.ones((VT_ROWS - HEAD_DIM, seq + (nb - 1) * t), BF16)
            vt_ref[g, :HEAD_DIM, pl.ds(seq, (nb - 1) * t)] = jnp.zeros((HEAD_DIM, (nb - 1) * t), BF16)
            k32 = k_ref[:, _head_cols(g)].astype(F32)
            kn2 = jnp.max(jnp.sum(k32 * k32, axis=1, keepdims=True), axis=0, keepdims=True)
            knorm_ref[g] = jnp.broadcast_to(kn2, knorm_ref.shape[1:])
            km = jnp.mean(k32.reshape(n_blocks, t, HEAD_DIM), axis=1)
            hi = km.astype(BF16)
            kmh_ref[g] = hi
            kml_ref[g] = (km - hi.astype(F32)).astype(BF16)
            kbias_ref[g] = row.astype(F32) * slope_log2[g]

    blk = lax.broadcasted_iota(jnp.int32, (n_blocks, t), 0)
    valid = blk < i

    def scores(j, g):
        k0 = pl.multiple_of(j * t, t)
        return _nt_dot(k_ref[pl.ds(k0, t), _head_cols(g)], q_ref[:, _head_cols(g)])

    def weighted_sum(j, g, slot, alpha):
        total = alpha * acc_ref[g]
        for h in range(nb):
            k0 = pl.multiple_of((j + h) * t, t)
            total = total + jnp.dot(vt_ref[g, :, pl.ds(k0, t)], p_refs[g][slot, h],
                                    preferred_element_type=F32)
        acc_ref[g] = total

    ms, own_ps, keeps = [], [], []
    for g in heads:
        q = q_ref[:, _head_cols(g)]
        gate = _nt_dot(kmh_ref[g], q) + _nt_dot(kml_ref[g], q)
        gate = jnp.where(valid, gate, NEG)
        chosen = jnp.zeros((n_blocks, t), jnp.bool_)
        for _ in range(MOBA_TOPK):
            top = jnp.max(gate, axis=0, keepdims=True)
            first = jnp.min(jnp.where(gate == top, blk, n_blocks), axis=0, keepdims=True)
            pick = blk == first
            chosen = jnp.logical_or(chosen, pick)
            gate = jnp.where(pick, -jnp.inf, gate)
        selb_ref[g] = jnp.where(jnp.logical_and(chosen, valid), 0.0, NEG)

        x = jnp.where(col >= row, scores(i, g) + kbias_ref[g], NEG)
        m = jnp.max(x, axis=0, keepdims=True)
        ms.append(m)
        own_ps.append(jnp.exp2(x - m).astype(BF16))
        acc_ref[g] = jnp.zeros((VT_ROWS, t), F32)

        q32 = q.astype(F32)
        qn2 = jnp.max(jnp.sum(q32 * q32, axis=1, keepdims=True), axis=0, keepdims=True)
        reach = jnp.sqrt(qn2 * knorm_ref[g, :1, :1]) - jnp.min(m, axis=1, keepdims=True)
        keeps.append((reach - ZERO_WEIGHT_LOG2) * inv_span[g] + 2.0)

    keep = keeps[0]
    for other in keeps[1:]:
        keep = jnp.maximum(keep, other)
    keep = jnp.clip(keep, 0.0, float(n_blocks)).astype(jnp.int32)[0, 0]
    first_step = jnp.maximum(i - keep, 0) // nb
    for g in heads:
        p_refs[g][first_step & 1, 0] = own_ps[g]
        for h in range(1, nb):
            p_refs[g][first_step & 1, h] = jnp.zeros((t, t), BF16)

    def body(step, state):
        prev, state = state[0], state[1:]
        alphas, ms = state[:n_heads], state[n_heads:]
        slot = step & 1
        j = nb * step
        for g in heads:
            weighted_sum(prev, g, slot, alphas[g])
        dots = [[scores(j + h, g) for h in range(nb)] for g in heads]
        new_alphas, new_ms = [], []
        for g in heads:
            base = slope_log2[g] * ((j - i) * t).astype(F32)
            xs = [dots[g][h] + kbias_ref[g] for h in range(nb)]
            shifts = [(base + slope_log2[g] * (h * t)) + selb_ref[g, pl.ds(j + h, 1), :]
                      for h in range(nb)]
            m_new = ms[g]
            for h in range(nb):
                m_new = jnp.maximum(m_new, jnp.max(xs[h], axis=0, keepdims=True) + shifts[h])
            new_alphas.append(jnp.exp2(ms[g] - m_new))
            for h in range(nb):
                p_refs[g][1 - slot, h] = jnp.exp2(xs[h] - (m_new - shifts[h])).astype(BF16)
            new_ms.append(m_new)
        return (j, *new_alphas, *new_ms)

    n_steps = (i + nb - 1) // nb
    ones = [jnp.ones((1, t), F32) for _ in heads]
    state = lax.fori_loop(first_step, n_steps, body, (i, *ones, *ms))
    prev, alphas = state[0], state[1:1 + n_heads]
    for g in heads:
        weighted_sum(prev, g, n_steps & 1, alphas[g])
        acc = acc_ref[g]
        o_ref[:, _head_cols(g)] = (acc[:HEAD_DIM] / acc[HEAD_DIM:HEAD_DIM + 1]).T.astype(o_ref.dtype)


def moba_attention(qk, p2, slopes, batch, seq):
    t = ATT_TILE
    nq = seq // t
    n_blocks = seq // MOBA_BLOCK
    grid_spec = pltpu.PrefetchScalarGridSpec(
        num_scalar_prefetch=2,
        grid=(batch, N_GROUPS, nq),
        in_specs=[pl.BlockSpec((t, GROUP_WIDTH), lambda b, h, i, s, r: (b * nq + i, h)),
                  pl.BlockSpec((seq, GROUP_WIDTH), lambda b, h, i, s, r: (b, N_GROUPS + h)),
                  pl.BlockSpec((seq, GROUP_WIDTH), lambda b, h, i, s, r: (b, h))],
        out_specs=pl.BlockSpec((t, GROUP_WIDTH), lambda b, h, i, s, r: (b * nq + i, h)),
        scratch_shapes=[pltpu.VMEM((HEADS_PER_STEP, VT_ROWS, seq + (MOBA_STEP_BLOCKS - 1) * t), BF16),
                        pltpu.VMEM((HEADS_PER_STEP, n_blocks, HEAD_DIM), BF16),
                        pltpu.VMEM((HEADS_PER_STEP, n_blocks, HEAD_DIM), BF16),
                        pltpu.VMEM((HEADS_PER_STEP, t, t), F32),
                        pltpu.VMEM((HEADS_PER_STEP, n_blocks, t), F32),
                        pltpu.VMEM((HEADS_PER_STEP, VT_ROWS, t), F32),
                        pltpu.VMEM((HEADS_PER_STEP, 8, HEAD_DIM), F32)]
                       + [pltpu.VMEM((2, MOBA_STEP_BLOCKS, t, t), BF16) for _ in range(HEADS_PER_STEP)],
    )
    return pl.pallas_call(
        functools.partial(_moba_kernel, n_blocks=n_blocks),
        grid_spec=grid_spec,
        out_shape=jax.ShapeDtypeStruct((batch * seq, WIDTH), BF16),
        compiler_params=_params(("arbitrary", "arbitrary", "arbitrary")),
        name="moba_attention",
    )(slopes, 1.0 / (slopes * (LOG2E * MOBA_BLOCK)), qk, qk, p2)


def _layer(x, layer, w_in, weights, vectors, slopes, batch, seq):
    g_mix, b_gate, g_q, g_k, g_mlp = vectors
    depth = weights["w_in"].shape[0]
    h = rms_norm(x, g_mix)
    qk_gain = jnp.concatenate([jnp.tile(g_q * SCORE_LOG2, N_HEADS),
                               jnp.tile(g_k, N_HEADS)]).reshape(1, 2 * WIDTH)
    qk, w_out, w_bm, w_bs = matmul(
        _mm_headnorm_kernel, h, w_in, [(qk_gain, "row")], n=2 * WIDTH, col_off=0,
        out_dtype=BF16, tm=1024, tn=1024, name="in_proj_qk_moba",
        side_casts=[(weights["w_out"], layer), (weights["w_branch_moba"], layer),
                    (weights["w_branch_sb"], layer)])
    rest_scale = jnp.concatenate([jnp.ones((WIDTH,), F32), jnp.full((WIDTH,), SCORE_LOG2, F32),
                                  jnp.ones((2 * WIDTH,), F32)]).reshape(1, 4 * WIDTH)
    p2, w_up = matmul(_mm_colscale_kernel, h, w_in, [(rest_scale, "row")], n=4 * WIDTH,
                      col_off=2 * WIDTH, out_dtype=BF16, tm=1024, tn=1024, name="in_proj_rest",
                      side_casts=[(weights["w_up"], layer)])
    gates = matmul(_mm_sigmoid_kernel, h, w_in, [(b_gate.reshape(1, -1), "row")], n=2 * D_MODEL,
                   col_off=6 * WIDTH, out_dtype=F32, tm=1024, tn=1024, name="in_proj_gates")
    ya = moba_attention(qk, p2, slopes, batch, seq)
    yb = sb_attention(p2, batch, seq)
    merged = merge_branches(ya, yb, w_bm, w_bs, gates)
    x = matmul(_mm_residual_kernel, merged, w_out, [(x, "tile")], n=D_MODEL, col_off=0,
               out_dtype=F32, tm=1024, tn=1024, name="out_proj")
    h2 = rms_norm(x, g_mlp)
    up_casts = [(weights["w_down"], layer)]
    if layer + 1 < depth:
        up_casts.append((weights["w_in"], layer + 1))
    u, w_down, *next_w_in = matmul(_mm_relu2_kernel, h2, w_up, [], n=D_FF, col_off=0,
                                   out_dtype=BF16, tm=1024, tn=1024, name="mlp_up",
                                   side_casts=up_casts)
    x = matmul_kgrid_residual(u, w_down, x, tm=1024, tn=1024, tk=2048, name="mlp_down")
    return x, (next_w_in[0] if next_w_in else None)


def kernel(x, norm_mix, w_in, b_gate, q_norm, k_norm, w_branch_moba, w_branch_sb, w_out,
           norm_mlp, w_up, w_down):
    batch, seq, d = x.shape
    depth = w_in.shape[0]
    slopes = jnp.exp2(-8.0 * jnp.arange(1, N_HEADS + 1, dtype=F32) / N_HEADS)
    weights = dict(w_in=w_in, w_branch_moba=w_branch_moba, w_branch_sb=w_branch_sb, w_out=w_out,
                   w_up=w_up, w_down=w_down)
    y = x.reshape(batch * seq, d)
    layer_w_in = layer_weight_bf16(w_in, 0)
    for l in range(depth):
        vectors = (norm_mix[l], b_gate[l], q_norm[l], k_norm[l], norm_mlp[l])
        y, layer_w_in = _layer(y, l, layer_w_in, weights, vectors, slopes, batch, seq)
    return y.reshape(batch, seq, d)
```

```python
import functools

import jax
import jax.numpy as jnp
from jax import lax
from jax.experimental import pallas as pl
from jax.experimental.pallas import tpu as pltpu

D_MODEL = 4096
HEAD_DIM = 128
N_HEADS = 16
WIDTH = N_HEADS * HEAD_DIM
MOBA_BLOCK = 256
MOBA_TOPK = 3
D_FF = 4 * D_MODEL
RMS_EPS = 1e-6
NEG = -1e30
SCALE = HEAD_DIM ** -0.5

VMEM_LIMIT_BYTES = 56 * 1024 * 1024

F32 = jnp.float32
BF16 = jnp.bfloat16


def _params(semantics):
    return pltpu.CompilerParams(dimension_semantics=semantics, vmem_limit_bytes=VMEM_LIMIT_BYTES)


def _rms_kernel(x_ref, g_ref, o_ref):
    x = x_ref[...]
    ms = jnp.mean(x * x, axis=-1, keepdims=True)
    o_ref[...] = ((x * lax.rsqrt(ms + RMS_EPS)) * g_ref[...]).astype(o_ref.dtype)


def rms_norm(x, g, *, tm=256):
    m, d = x.shape
    return pl.pallas_call(
        _rms_kernel,
        grid=(m // tm,),
        in_specs=[pl.BlockSpec((tm, d), lambda i: (i, 0)),
                  pl.BlockSpec((1, d), lambda i: (0, 0))],
        out_specs=pl.BlockSpec((tm, d), lambda i: (i, 0)),
        out_shape=jax.ShapeDtypeStruct((m, d), BF16),
        compiler_params=_params(("arbitrary",)),
        name="rms_norm",
    )(x, g.reshape(1, d))


def _cast_kernel(w_ref, o_ref):
    o_ref[...] = w_ref[...].astype(o_ref.dtype)


def layer_weight_bf16(w, layer, *, tr=512, tc=4096):
    _, rows, cols = w.shape
    tc = min(tc, cols)
    return pl.pallas_call(
        _cast_kernel,
        grid=(rows // tr, cols // tc),
        in_specs=[pl.BlockSpec((None, tr, tc), lambda i, j: (layer, i, j))],
        out_specs=pl.BlockSpec((tr, tc), lambda i, j: (i, j)),
        out_shape=jax.ShapeDtypeStruct((rows, cols), BF16),
        compiler_params=_params(("arbitrary", "arbitrary")),
        name="weight_cast",
    )(w)


def _mm_colscale_kernel(a_ref, w_ref, s_ref, o_ref):
    acc = jnp.dot(a_ref[...], w_ref[...], preferred_element_type=F32)
    o_ref[...] = (acc * s_ref[...]).astype(o_ref.dtype)


def _mm_headnorm_kernel(a_ref, w_ref, g_ref, o_ref):
    acc = jnp.dot(a_ref[...], w_ref[...], preferred_element_type=F32)
    for c in range(acc.shape[1] // HEAD_DIM):
        sl = slice(c * HEAD_DIM, (c + 1) * HEAD_DIM)
        y = acc[:, sl]
        ms = jnp.mean(y * y, axis=-1, keepdims=True)
        o_ref[:, sl] = ((y * lax.rsqrt(ms + RMS_EPS)) * g_ref[:, sl]).astype(o_ref.dtype)


def _mm_sigmoid_kernel(a_ref, w_ref, b_ref, o_ref):
    acc = jnp.dot(a_ref[...], w_ref[...], preferred_element_type=F32)
    o_ref[...] = (0.5 * jnp.tanh(0.5 * (acc + b_ref[...])) + 0.5).astype(o_ref.dtype)


def _mm_relu2_kernel(a_ref, w_ref, o_ref):
    acc = jnp.dot(a_ref[...], w_ref[...], preferred_element_type=F32)
    o_ref[...] = jnp.square(jnp.maximum(acc, 0.0)).astype(o_ref.dtype)


def _mm_residual_kernel(a_ref, w_ref, r_ref, o_ref):
    acc = jnp.dot(a_ref[...], w_ref[...], preferred_element_type=F32)
    o_ref[...] = r_ref[...] + acc


def _with_side_casts(kernel, n_in, n_side, *refs):
    ins, side_ins = refs[:n_in], refs[n_in:n_in + n_side]
    out, side_outs = refs[n_in + n_side], refs[n_in + n_side + 1:]
    kernel(*ins, out)
    for src, dst in zip(side_ins, side_outs):
        dst[...] = src[...].astype(dst.dtype)


def _side_cast_specs(side_casts, grid):
    gi, gj = grid
    in_specs, out_specs, out_shapes, arrays = [], [], [], []
    for w, layer in side_casts:
        _, rows, cols = w.shape
        br, bc = rows // gi, cols // gj
        assert br * gi == rows and bc * gj == cols and br % 16 == 0 and bc % 128 == 0
        in_specs.append(pl.BlockSpec((None, br, bc), lambda i, j, layer=layer: (layer, i, j)))
        out_specs.append(pl.BlockSpec((br, bc), lambda i, j: (i, j)))
        out_shapes.append(jax.ShapeDtypeStruct((rows, cols), BF16))
        arrays.append(w)
    return in_specs, out_specs, out_shapes, arrays


def matmul(kernel, a, w, extras, *, n, col_off, out_dtype, tm, tn, name, side_casts=()):
    m, k = a.shape
    off = col_off // tn
    assert off * tn == col_off and n % tn == 0 and m % tm == 0
    grid = (m // tm, n // tn)
    in_specs = [pl.BlockSpec((tm, k), lambda i, j: (i, 0)),
                pl.BlockSpec((k, tn), lambda i, j: (0, j + off))]
    args = [a, w]
    for arr, kind in extras:
        if kind == "row":
            in_specs.append(pl.BlockSpec((1, tn), lambda i, j: (0, j)))
        else:
            in_specs.append(pl.BlockSpec((tm, tn), lambda i, j: (i, j)))
        args.append(arr)
    side_in, side_out, side_shapes, side_arrays = _side_cast_specs(side_casts, grid)
    outs = pl.pallas_call(
        functools.partial(_with_side_casts, kernel, len(args), len(side_arrays)),
        grid=grid,
        in_specs=in_specs + side_in,
        out_specs=[pl.BlockSpec((tm, tn), lambda i, j: (i, j))] + side_out,
        out_shape=[jax.ShapeDtypeStruct((m, n), out_dtype)] + side_shapes,
        compiler_params=_params(("arbitrary", "arbitrary")),
        name=name,
    )(*args, *side_arrays)
    return outs if side_casts else outs[0]


def _mm_kgrid_residual_kernel(a_ref, w_ref, r_ref, o_ref, acc_ref):
    kk = pl.program_id(2)

    @pl.when(kk == 0)
    def _():
        acc_ref[...] = jnp.zeros_like(acc_ref)

    acc_ref[...] += jnp.dot(a_ref[...], w_ref[...], preferred_element_type=F32)

    @pl.when(kk == pl.num_programs(2) - 1)
    def _():
        o_ref[...] = r_ref[...] + acc_ref[...]


def matmul_kgrid_residual(a, w, r, *, tm, tn, tk, name):
    m, k = a.shape
    n = w.shape[1]
    return pl.pallas_call(
        _mm_kgrid_residual_kernel,
        grid=(m // tm, n // tn, k // tk),
        in_specs=[pl.BlockSpec((tm, tk), lambda i, j, kk: (i, kk)),
                  pl.BlockSpec((tk, tn), lambda i, j, kk: (kk, j)),
                  pl.BlockSpec((tm, tn), lambda i, j, kk: (i, j))],
        out_specs=pl.BlockSpec((tm, tn), lambda i, j, kk: (i, j)),
        out_shape=jax.ShapeDtypeStruct((m, n), F32),
        scratch_shapes=[pltpu.VMEM((tm, tn), F32)],
        compiler_params=_params(("arbitrary", "arbitrary", "arbitrary")),
        name=name,
    )(a, w, r)


def _merge_kernel(ya_ref, yb_ref, wa_ref, wb_ref, ga_ref, gb_ref, o_ref):
    pa = jnp.dot(ya_ref[...], wa_ref[...], preferred_element_type=F32)
    pb = jnp.dot(yb_ref[...], wb_ref[...], preferred_element_type=F32)
    o_ref[...] = (ga_ref[...] * pa + gb_ref[...] * pb).astype(o_ref.dtype)


def merge_branches(ya, yb, wa, wb, gates, *, tm=1024, tn=512, side_casts=()):
    m, k = ya.shape
    n = wa.shape[1]
    goff = n // tn
    grid = (m // tm, n // tn)
    side_in, side_out, side_shapes, side_arrays = _side_cast_specs(side_casts, grid)
    outs = pl.pallas_call(
        functools.partial(_with_side_casts, _merge_kernel, 6, len(side_arrays)),
        grid=grid,
        in_specs=[pl.BlockSpec((tm, k), lambda i, j: (i, 0)),
                  pl.BlockSpec((tm, k), lambda i, j: (i, 0)),
                  pl.BlockSpec((k, tn), lambda i, j: (0, j)),
                  pl.BlockSpec((k, tn), lambda i, j: (0, j)),
                  pl.BlockSpec((tm, tn), lambda i, j: (i, j)),
                  pl.BlockSpec((tm, tn), lambda i, j: (i, j + goff))] + side_in,
        out_specs=[pl.BlockSpec((tm, tn), lambda i, j: (i, j))] + side_out,
        out_shape=[jax.ShapeDtypeStruct((m, n), BF16)] + side_shapes,
        compiler_params=_params(("arbitrary", "arbitrary")),
        name="merge_branches",
    )(ya, yb, wa, wb, gates, gates, *side_arrays)
    return outs if side_casts else outs[0]


ATT_TILE = 256
HEADS_PER_STEP = 4
GROUP_WIDTH = HEADS_PER_STEP * HEAD_DIM
N_GROUPS = N_HEADS // HEADS_PER_STEP

LOG2E = 1.4426950408889634
SCORE_LOG2 = SCALE * LOG2E
ZERO_WEIGHT_LOG2 = -160.0
VT_ROWS = HEAD_DIM + 16
MOBA_STEP_BLOCKS = 2


def _head_cols(g):
    return slice(g * HEAD_DIM, (g + 1) * HEAD_DIM)


def _transpose_values(v_ref, vt_ref, g):
    n_chunks = v_ref.shape[0] // ATT_TILE

    def body(c, carry):
        r0 = pl.multiple_of(c * ATT_TILE, ATT_TILE)
        vt_ref[g, :HEAD_DIM, pl.ds(r0, ATT_TILE)] = (
            v_ref[pl.ds(r0, ATT_TILE), _head_cols(g)].astype(F32).T.astype(BF16))
        return carry

    lax.fori_loop(0, n_chunks, body, 0, unroll=8)


def _nt_dot(a, b):
    return lax.dot_general(a, b, (((1,), (1,)), ((), ())), preferred_element_type=F32)


def _neg_abs(x):
    bits = lax.bitcast_convert_type(x, jnp.uint32) | jnp.uint32(0x80000000)
    return lax.bitcast_convert_type(bits, F32)


SB_TILES = 2


def _sb_kernel(q_ref, k_ref, v_ref, o_ref, vt_ref, acc_ref):
    ip = pl.program_id(2)
    t = ATT_TILE
    tasks = [(a, g) for a in range(SB_TILES) for g in range(HEADS_PER_STEP)]

    @pl.when(ip == 0)
    def _():
        for g in range(HEADS_PER_STEP):
            _transpose_values(v_ref, vt_ref, g)

    row = lax.broadcasted_iota(jnp.int32, (t, t), 0)
    col = lax.broadcasted_iota(jnp.int32, (t, t), 1)
    upper = (col > row).astype(BF16)

    def block(blocks, carries, diagonal):
        starts = [pl.multiple_of(jnp.maximum(j, 0) * t, t) for j in blocks]
        zs = [_nt_dot(k_ref[pl.ds(starts[a], t), _head_cols(g)],
                      q_ref[a * t:(a + 1) * t, _head_cols(g)]) for a, g in tasks]
        log_betas, log_1ms, cums = [], [], []
        for n in range(len(tasks)):
            z = zs[n]
            log_beta = jnp.minimum(z, 0.0) - jnp.log2(1.0 + jnp.exp2(_neg_abs(z)))
            log_1m = log_beta - z
            if diagonal:
                log_1m = jnp.where(row < col, log_1m, 0.0)
            hi = log_1m.astype(BF16)
            lo = (log_1m - hi.astype(F32)).astype(BF16)
            cums.append(jnp.dot(upper, hi, preferred_element_type=F32)
                        + jnp.dot(upper, lo, preferred_element_type=F32))
            log_betas.append(log_beta)
            log_1ms.append(log_1m)
        pvs, new_carries = [], []
        for n, (a, g) in enumerate(tasks):
            w = jnp.exp2(log_betas[n] + (cums[n] + carries[n]))
            if diagonal:
                w = jnp.where(row < col, w, 0.0)
            pvs.append(jnp.dot(vt_ref[g, :, pl.ds(starts[a], t)], w.astype(BF16),
                               preferred_element_type=F32))
            carry = carries[n] + jnp.sum(log_1ms[n], axis=0, keepdims=True)
            new_carries.append(jnp.where(blocks[a] <= 0, NEG, carry))
        return new_carries, pvs

    def any_alive(carries):
        top = carries[0]
        for c in carries[1:]:
            top = jnp.maximum(top, c)
        return jnp.max(top, axis=1, keepdims=True)[0, 0] > ZERO_WEIGHT_LOG2

    first = [SB_TILES * ip + a for a in range(SB_TILES)]
    carries, pvs = block(first, [jnp.zeros((1, t), F32) for _ in tasks], True)
    for n in range(len(tasks)):
        acc_ref[n] = pvs[n]

    def cond(state):
        step, alive = state[0], state[1]
        return jnp.logical_and(step < first[-1], alive)

    def body(state):
        step = state[0]
        carries, pvs = block([j - 1 - step for j in first], list(state[2:]), False)
        for n in range(len(tasks)):
            acc_ref[n] += pvs[n]
        return (step + 1, any_alive(carries), *carries)

    lax.while_loop(cond, body, (jnp.int32(0), any_alive(carries), *carries))
    for n, (a, g) in enumerate(tasks):
        o_ref[a * t:(a + 1) * t, _head_cols(g)] = acc_ref[n].T.astype(o_ref.dtype)


def sb_attention(p2, batch, seq):
    t = ATT_TILE
    rows = SB_TILES * t
    nq = seq // rows
    qc, kc, vc = N_GROUPS, 2 * N_GROUPS, 3 * N_GROUPS
    return pl.pallas_call(
        _sb_kernel,
        grid=(batch, N_GROUPS, nq),
        in_specs=[pl.BlockSpec((rows, GROUP_WIDTH), lambda b, h, i: (b * nq + i, qc + h)),
                  pl.BlockSpec((seq, GROUP_WIDTH), lambda b, h, i: (b, kc + h)),
                  pl.BlockSpec((seq, GROUP_WIDTH), lambda b, h, i: (b, vc + h))],
        out_specs=pl.BlockSpec((rows, GROUP_WIDTH), lambda b, h, i: (b * nq + i, h)),
        out_shape=jax.ShapeDtypeStruct((batch * seq, WIDTH), BF16),
        scratch_shapes=[pltpu.VMEM((HEADS_PER_STEP, HEAD_DIM, seq), BF16),
                        pltpu.VMEM((SB_TILES * HEADS_PER_STEP, HEAD_DIM, t), F32)],
        compiler_params=_params(("arbitrary", "arbitrary", "arbitrary")),
        name="sb_attention",
    )(p2, p2, p2)


MOBA_TILES = 2


def _moba_kernel(slopes_ref, inv_span_ref, q_ref, k_ref, v_ref, o_ref, vt_ref, kmh_ref, kml_ref,
                 kbias_ref, selb_ref, acc_ref, knorm_ref, *p_refs, n_blocks):
    hg = pl.program_id(1)
    ip = pl.program_id(2)
    t = ATT_TILE
    heads = range(HEADS_PER_STEP)
    tasks = [(a, g) for a in range(MOBA_TILES) for g in heads]
    n_tasks = len(tasks)
    nb = MOBA_STEP_BLOCKS
    seq = k_ref.shape[0]
    tile_block = [MOBA_TILES * ip + a for a in range(MOBA_TILES)]

    row = lax.broadcasted_iota(jnp.int32, (t, t), 0)
    col = lax.broadcasted_iota(jnp.int32, (t, t), 1)
    slope_log2 = [slopes_ref[hg * HEADS_PER_STEP + g] * LOG2E for g in heads]
    inv_span = [inv_span_ref[hg * HEADS_PER_STEP + g] for g in heads]

    @pl.when(ip == 0)
    def _():
        for g in heads:
            _transpose_values(v_ref, vt_ref, g)
            vt_ref[g, HEAD_DIM:, :] = jnp.ones((VT_ROWS - HEAD_DIM, seq + nb * t), BF16)
            vt_ref[g, :HEAD_DIM, pl.ds(seq, nb * t)] = jnp.zeros((HEAD_DIM, nb * t), BF16)
            k32 = k_ref[:, _head_cols(g)].astype(F32)
            kn2 = jnp.max(jnp.sum(k32 * k32, axis=1, keepdims=True), axis=0, keepdims=True)
            knorm_ref[g] = jnp.broadcast_to(kn2, knorm_ref.shape[1:])
            km = jnp.mean(k32.reshape(n_blocks, t, HEAD_DIM), axis=1)
            hi = km.astype(BF16)
            kmh_ref[g] = hi
            kml_ref[g] = (km - hi.astype(F32)).astype(BF16)
            kbias_ref[g] = row.astype(F32) * slope_log2[g]

    blk = lax.broadcasted_iota(jnp.int32, (n_blocks, t), 0)

    def query(a, g):
        return q_ref[a * t:(a + 1) * t, _head_cols(g)]

    def scores(j, a, g):
        k0 = pl.multiple_of(jnp.minimum(j, n_blocks - 1) * t, t)
        return _nt_dot(k_ref[pl.ds(k0, t), _head_cols(g)], query(a, g))

    def weighted_sum(j, n, slot, alpha):
        g = tasks[n][1]
        total = alpha * acc_ref[n]
        for h in range(nb):
            k0 = pl.multiple_of((j + h) * t, t)
            total = total + jnp.dot(vt_ref[g, :, pl.ds(k0, t)], p_refs[n][slot, h],
                                    preferred_element_type=F32)
        acc_ref[n] = total

    ms, own_ps, keeps = [], [], []
    for n, (a, g) in enumerate(tasks):
        q = query(a, g)
        valid = blk < tile_block[a]
        gate = _nt_dot(kmh_ref[g], q) + _nt_dot(kml_ref[g], q)
        gate = jnp.where(valid, gate, NEG)
        chosen = jnp.zeros((n_blocks, t), jnp.bool_)
        for _ in range(MOBA_TOPK):
            top = jnp.max(gate, axis=0, keepdims=True)
            first = jnp.min(jnp.where(gate == top, blk, n_blocks), axis=0, keepdims=True)
            pick = blk == first
            chosen = jnp.logical_or(chosen, pick)
            gate = jnp.where(pick, -jnp.inf, gate)
        selb_ref[n] = jnp.where(jnp.logical_and(chosen, valid), 0.0, NEG)

        x = jnp.where(col >= row, scores(tile_block[a], a, g) + kbias_ref[g], NEG)
        m = jnp.max(x, axis=0, keepdims=True)
        ms.append(m)
        own_ps.append(jnp.exp2(x - m).astype(BF16))
        acc_ref[n] = jnp.zeros((VT_ROWS, t), F32)

        q32 = q.astype(F32)
        qn2 = jnp.max(jnp.sum(q32 * q32, axis=1, keepdims=True), axis=0, keepdims=True)
        reach = jnp.sqrt(qn2 * knorm_ref[g, :1, :1]) - jnp.min(m, axis=1, keepdims=True)
        keeps.append((reach - ZERO_WEIGHT_LOG2) * inv_span[g] + 2.0)

    keep = keeps[0]
    for other in keeps[1:]:
        keep = jnp.maximum(keep, other)
    keep = jnp.clip(keep, 0.0, float(n_blocks)).astype(jnp.int32)[0, 0]
    first_step = jnp.maximum(tile_block[0] - keep, 0) // nb
    for n in range(n_tasks):
        p_refs[n][first_step & 1, 0] = own_ps[n]
        for h in range(1, nb):
            p_refs[n][first_step & 1, h] = jnp.zeros((t, t), BF16)

    def body(step, state):
        prevs, state = state[:MOBA_TILES], state[MOBA_TILES:]
        alphas, ms = state[:n_tasks], state[n_tasks:]
        slot = step & 1
        j = nb * step
        for n, (a, g) in enumerate(tasks):
            weighted_sum(prevs[a], n, slot, alphas[n])
        dots = [[scores(j + h, a, g) for h in range(nb)] for a, g in tasks]
        new_alphas, new_ms = [], []
        for n, (a, g) in enumerate(tasks):
            base = slope_log2[g] * ((j - tile_block[a]) * t).astype(F32)
            xs = [dots[n][h] + kbias_ref[g] for h in range(nb)]
            shifts = [(base + slope_log2[g] * (h * t))
                      + selb_ref[n, pl.ds(jnp.minimum(j + h, n_blocks - 1), 1), :] for h in range(nb)]
            m_new = ms[n]
            for h in range(nb):
                m_new = jnp.maximum(m_new, jnp.max(xs[h], axis=0, keepdims=True) + shifts[h])
            new_alphas.append(jnp.exp2(ms[n] - m_new))
            for h in range(nb):
                p_refs[n][1 - slot, h] = jnp.exp2(xs[h] - (m_new - shifts[h])).astype(BF16)
            new_ms.append(m_new)
        return (*([j] * MOBA_TILES), *new_alphas, *new_ms)

    n_steps = (tile_block[-1] + nb - 1) // nb
    ones = [jnp.ones((1, t), F32) for _ in tasks]
    state = lax.fori_loop(first_step, n_steps, body, (*tile_block, *ones, *ms))
    prevs, alphas = state[:MOBA_TILES], state[MOBA_TILES:MOBA_TILES + n_tasks]
    for n, (a, g) in enumerate(tasks):
        weighted_sum(prevs[a], n, n_steps & 1, alphas[n])
        acc = acc_ref[n]
        o_ref[a * t:(a + 1) * t, _head_cols(g)] = (
            acc[:HEAD_DIM] / acc[HEAD_DIM:HEAD_DIM + 1]).T.astype(o_ref.dtype)


def moba_attention(qk, p2, slopes, batch, seq):
    t = ATT_TILE
    rows = MOBA_TILES * t
    nq = seq // rows
    n_blocks = seq // MOBA_BLOCK
    n_tasks = MOBA_TILES * HEADS_PER_STEP
    grid_spec = pltpu.PrefetchScalarGridSpec(
        num_scalar_prefetch=2,
        grid=(batch, N_GROUPS, nq),
        in_specs=[pl.BlockSpec((rows, GROUP_WIDTH), lambda b, h, i, s, r: (b * nq + i, h)),
                  pl.BlockSpec((seq, GROUP_WIDTH), lambda b, h, i, s, r: (b, N_GROUPS + h)),
                  pl.BlockSpec((seq, GROUP_WIDTH), lambda b, h, i, s, r: (b, h))],
        out_specs=pl.BlockSpec((rows, GROUP_WIDTH), lambda b, h, i, s, r: (b * nq + i, h)),
        scratch_shapes=[pltpu.VMEM((HEADS_PER_STEP, VT_ROWS, seq + MOBA_STEP_BLOCKS * t), BF16),
                        pltpu.VMEM((HEADS_PER_STEP, n_blocks, HEAD_DIM), BF16),
                        pltpu.VMEM((HEADS_PER_STEP, n_blocks, HEAD_DIM), BF16),
                        pltpu.VMEM((HEADS_PER_STEP, t, t), F32),
                        pltpu.VMEM((n_tasks, n_blocks, t), F32),
                        pltpu.VMEM((n_tasks, VT_ROWS, t), F32),
                        pltpu.VMEM((HEADS_PER_STEP, 8, HEAD_DIM), F32)]
                       + [pltpu.VMEM((2, MOBA_STEP_BLOCKS, t, t), BF16) for _ in range(n_tasks)],
    )
    return pl.pallas_call(
        functools.partial(_moba_kernel, n_blocks=n_blocks),
        grid_spec=grid_spec,
        out_shape=jax.ShapeDtypeStruct((batch * seq, WIDTH), BF16),
        compiler_params=_params(("arbitrary", "arbitrary", "arbitrary")),
        name="moba_attention",
    )(slopes, 1.0 / (slopes * (LOG2E * MOBA_BLOCK)), qk, qk, p2)


def _layer(x, layer, w_in, weights, vectors, slopes, batch, seq):
    g_mix, b_gate, g_q, g_k, g_mlp = vectors
    depth = weights["w_in"].shape[0]
    h = rms_norm(x, g_mix)
    qk_gain = jnp.concatenate([jnp.tile(g_q * SCORE_LOG2, N_HEADS),
                               jnp.tile(g_k, N_HEADS)]).reshape(1, 2 * WIDTH)
    qk, w_out, w_bm, w_bs = matmul(
        _mm_headnorm_kernel, h, w_in, [(qk_gain, "row")], n=2 * WIDTH, col_off=0,
        out_dtype=BF16, tm=1024, tn=1024, name="in_proj_qk_moba",
        side_casts=[(weights["w_out"], layer), (weights["w_branch_moba"], layer),
                    (weights["w_branch_sb"], layer)])
    rest_scale = jnp.concatenate([jnp.ones((WIDTH,), F32), jnp.full((WIDTH,), SCORE_LOG2, F32),
                                  jnp.ones((2 * WIDTH,), F32)]).reshape(1, 4 * WIDTH)
    p2, w_up = matmul(_mm_colscale_kernel, h, w_in, [(rest_scale, "row")], n=4 * WIDTH,
                      col_off=2 * WIDTH, out_dtype=BF16, tm=1024, tn=1024, name="in_proj_rest",
                      side_casts=[(weights["w_up"], layer)])
    gates = matmul(_mm_sigmoid_kernel, h, w_in, [(b_gate.reshape(1, -1), "row")], n=2 * D_MODEL,
                   col_off=6 * WIDTH, out_dtype=F32, tm=1024, tn=1024, name="in_proj_gates")
    ya = moba_attention(qk, p2, slopes, batch, seq)
    yb = sb_attention(p2, batch, seq)
    merged = merge_branches(ya, yb, w_bm, w_bs, gates)
    x = matmul(_mm_residual_kernel, merged, w_out, [(x, "tile")], n=D_MODEL, col_off=0,
               out_dtype=F32, tm=1024, tn=1024, name="out_proj")
    h2 = rms_norm(x, g_mlp)
    up_casts = [(weights["w_down"], layer)]
    if layer + 1 < depth:
        up_casts.append((weights["w_in"], layer + 1))
    u, w_down, *next_w_in = matmul(_mm_relu2_kernel, h2, w_up, [], n=D_FF, col_off=0,
                                   out_dtype=BF16, tm=1024, tn=1024, name="mlp_up",
                                   side_casts=up_casts)
    x = matmul_kgrid_residual(u, w_down, x, tm=1024, tn=1024, tk=2048, name="mlp_down")
    return x, (next_w_in[0] if next_w_in else None)


def kernel(x, norm_mix, w_in, b_gate, q_norm, k_norm, w_branch_moba, w_branch_sb, w_out,
           norm_mlp, w_up, w_down):
    batch, seq, d = x.shape
    depth = w_in.shape[0]
    slopes = jnp.exp2(-8.0 * jnp.arange(1, N_HEADS + 1, dtype=F32) / N_HEADS)
    weights = dict(w_in=w_in, w_branch_moba=w_branch_moba, w_branch_sb=w_branch_sb, w_out=w_out,
                   w_up=w_up, w_down=w_down)
    y = x.reshape(batch * seq, d)
    layer_w_in = layer_weight_bf16(w_in, 0)
    for l in range(depth):
        vectors = (norm_mix[l], b_gate[l], q_norm[l], k_norm[l], norm_mlp[l])
        y, layer_w_in = _layer(y, l, layer_w_in, weights, vectors, slopes, batch, seq)
    return y.reshape(batch, seq, d)
```

```python
import functools

import jax
import jax.numpy as jnp
from jax import lax
from jax.experimental import pallas as pl
from jax.experimental.pallas import tpu as pltpu

D_MODEL = 4096
HEAD_DIM = 128
N_HEADS = 16
WIDTH = N_HEADS * HEAD_DIM
MOBA_BLOCK = 256
MOBA_TOPK = 3
D_FF = 4 * D_MODEL
RMS_EPS = 1e-6
NEG = -1e30
SCALE = HEAD_DIM ** -0.5

VMEM_LIMIT_BYTES = 56 * 1024 * 1024

F32 = jnp.float32
BF16 = jnp.bfloat16


def _params(semantics):
    return pltpu.CompilerParams(dimension_semantics=semantics, vmem_limit_bytes=VMEM_LIMIT_BYTES)


def _rms_kernel(x_ref, g_ref, o_ref):
    x = x_ref[...]
    ms = jnp.mean(x * x, axis=-1, keepdims=True)
    o_ref[...] = ((x * lax.rsqrt(ms + RMS_EPS)) * g_ref[...]).astype(o_ref.dtype)


def rms_norm(x, g, *, tm=256):
    m, d = x.shape
    return pl.pallas_call(
        _rms_kernel,
        grid=(m // tm,),
        in_specs=[pl.BlockSpec((tm, d), lambda i: (i, 0)),
                  pl.BlockSpec((1, d), lambda i: (0, 0))],
        out_specs=pl.BlockSpec((tm, d), lambda i: (i, 0)),
        out_shape=jax.ShapeDtypeStruct((m, d), BF16),
        compiler_params=_params(("arbitrary",)),
        name="rms_norm",
    )(x, g.reshape(1, d))


def _cast_kernel(w_ref, o_ref):
    o_ref[...] = w_ref[...].astype(o_ref.dtype)


def layer_weight_bf16(w, layer, *, tr=512, tc=4096):
    _, rows, cols = w.shape
    tc = min(tc, cols)
    return pl.pallas_call(
        _cast_kernel,
        grid=(rows // tr, cols // tc),
        in_specs=[pl.BlockSpec((None, tr, tc), lambda i, j: (layer, i, j))],
        out_specs=pl.BlockSpec((tr, tc), lambda i, j: (i, j)),
        out_shape=jax.ShapeDtypeStruct((rows, cols), BF16),
        compiler_params=_params(("arbitrary", "arbitrary")),
        name="weight_cast",
    )(w)


def _mm_colscale_kernel(a_ref, w_ref, s_ref, o_ref):
    acc = jnp.dot(a_ref[...], w_ref[...], preferred_element_type=F32)
    o_ref[...] = (acc * s_ref[...]).astype(o_ref.dtype)


def _mm_headnorm_kernel(a_ref, w_ref, g_ref, o_ref):
    acc = jnp.dot(a_ref[...], w_ref[...], preferred_element_type=F32)
    for c in range(acc.shape[1] // HEAD_DIM):
        sl = slice(c * HEAD_DIM, (c + 1) * HEAD_DIM)
        y = acc[:, sl]
        ms = jnp.mean(y * y, axis=-1, keepdims=True)
        o_ref[:, sl] = ((y * lax.rsqrt(ms + RMS_EPS)) * g_ref[:, sl]).astype(o_ref.dtype)


def _mm_sigmoid_kernel(a_ref, w_ref, b_ref, o_ref):
    acc = jnp.dot(a_ref[...], w_ref[...], preferred_element_type=F32)
    o_ref[...] = (0.5 * jnp.tanh(0.5 * (acc + b_ref[...])) + 0.5).astype(o_ref.dtype)


def _mm_relu2_kernel(a_ref, w_ref, o_ref):
    acc = jnp.dot(a_ref[...], w_ref[...], preferred_element_type=F32)
    o_ref[...] = jnp.square(jnp.maximum(acc, 0.0)).astype(o_ref.dtype)


def _mm_residual_kernel(a_ref, w_ref, r_ref, o_ref):
    acc = jnp.dot(a_ref[...], w_ref[...], preferred_element_type=F32)
    o_ref[...] = r_ref[...] + acc


def _with_side_casts(kernel, n_in, n_side, *refs):
    ins, side_ins = refs[:n_in], refs[n_in:n_in + n_side]
    out, side_outs = refs[n_in + n_side], refs[n_in + n_side + 1:]
    kernel(*ins, out)
    for src, dst in zip(side_ins, side_outs):
        dst[...] = src[...].astype(dst.dtype)


def _side_cast_specs(side_casts, grid):
    gi, gj = grid
    in_specs, out_specs, out_shapes, arrays = [], [], [], []
    for w, layer in side_casts:
        _, rows, cols = w.shape
        br, bc = rows // gi, cols // gj
        assert br * gi == rows and bc * gj == cols and br % 16 == 0 and bc % 128 == 0
        in_specs.append(pl.BlockSpec((None, br, bc), lambda i, j, layer=layer: (layer, i, j)))
        out_specs.append(pl.BlockSpec((br, bc), lambda i, j: (i, j)))
        out_shapes.append(jax.ShapeDtypeStruct((rows, cols), BF16))
        arrays.append(w)
    return in_specs, out_specs, out_shapes, arrays


def matmul(kernel, a, w, extras, *, n, col_off, out_dtype, tm, tn, name, side_casts=()):
    m, k = a.shape
    off = col_off // tn
    assert off * tn == col_off and n % tn == 0 and m % tm == 0
    grid = (m // tm, n // tn)
    in_specs = [pl.BlockSpec((tm, k), lambda i, j: (i, 0)),
                pl.BlockSpec((k, tn), lambda i, j: (0, j + off))]
    args = [a, w]
    for arr, kind in extras:
        if kind == "row":
            in_specs.append(pl.BlockSpec((1, tn), lambda i, j: (0, j)))
        else:
            in_specs.append(pl.BlockSpec((tm, tn), lambda i, j: (i, j)))
        args.append(arr)
    side_in, side_out, side_shapes, side_arrays = _side_cast_specs(side_casts, grid)
    outs = pl.pallas_call(
        functools.partial(_with_side_casts, kernel, len(args), len(side_arrays)),
        grid=grid,
        in_specs=in_specs + side_in,
        out_specs=[pl.BlockSpec((tm, tn), lambda i, j: (i, j))] + side_out,
        out_shape=[jax.ShapeDtypeStruct((m, n), out_dtype)] + side_shapes,
        compiler_params=_params(("arbitrary", "arbitrary")),
        name=name,
    )(*args, *side_arrays)
    return outs if side_casts else outs[0]


def _mm_kgrid_residual_kernel(a_ref, w_ref, r_ref, o_ref, acc_ref):
    kk = pl.program_id(2)

    @pl.when(kk == 0)
    def _():
        acc_ref[...] = jnp.zeros_like(acc_ref)

    acc_ref[...] += jnp.dot(a_ref[...], w_ref[...], preferred_element_type=F32)

    @pl.when(kk == pl.num_programs(2) - 1)
    def _():
        o_ref[...] = r_ref[...] + acc_ref[...]


def matmul_kgrid_residual(a, w, r, *, tm, tn, tk, name):
    m, k = a.shape
    n = w.shape[1]
    return pl.pallas_call(
        _mm_kgrid_residual_kernel,
        grid=(m // tm, n // tn, k // tk),
        in_specs=[pl.BlockSpec((tm, tk), lambda i, j, kk: (i, kk)),
                  pl.BlockSpec((tk, tn), lambda i, j, kk: (kk, j)),
                  pl.BlockSpec((tm, tn), lambda i, j, kk: (i, j))],
        out_specs=pl.BlockSpec((tm, tn), lambda i, j, kk: (i, j)),
        out_shape=jax.ShapeDtypeStruct((m, n), F32),
        scratch_shapes=[pltpu.VMEM((tm, tn), F32)],
        compiler_params=_params(("arbitrary", "arbitrary", "arbitrary")),
        name=name,
    )(a, w, r)


def _merge_kernel(ya_ref, yb_ref, wa_ref, wb_ref, ga_ref, gb_ref, o_ref):
    pa = jnp.dot(ya_ref[...], wa_ref[...], preferred_element_type=F32)
    pb = jnp.dot(yb_ref[...], wb_ref[...], preferred_element_type=F32)
    o_ref[...] = (ga_ref[...] * pa + gb_ref[...] * pb).astype(o_ref.dtype)


def merge_branches(ya, yb, wa, wb, gates, *, tm=1024, tn=512, side_casts=()):
    m, k = ya.shape
    n = wa.shape[1]
    goff = n // tn
    grid = (m // tm, n // tn)
    side_in, side_out, side_shapes, side_arrays = _side_cast_specs(side_casts, grid)
    outs = pl.pallas_call(
        functools.partial(_with_side_casts, _merge_kernel, 6, len(side_arrays)),
        grid=grid,
        in_specs=[pl.BlockSpec((tm, k), lambda i, j: (i, 0)),
                  pl.BlockSpec((tm, k), lambda i, j: (i, 0)),
                  pl.BlockSpec((k, tn), lambda i, j: (0, j)),
                  pl.BlockSpec((k, tn), lambda i, j: (0, j)),
                  pl.BlockSpec((tm, tn), lambda i, j: (i, j)),
                  pl.BlockSpec((tm, tn), lambda i, j: (i, j + goff))] + side_in,
        out_specs=[pl.BlockSpec((tm, tn), lambda i, j: (i, j))] + side_out,
        out_shape=[jax.ShapeDtypeStruct((m, n), BF16)] + side_shapes,
        compiler_params=_params(("arbitrary", "arbitrary")),
        name="merge_branches",
    )(ya, yb, wa, wb, gates, gates, *side_arrays)
    return outs if side_casts else outs[0]


ATT_TILE = 256
HEADS_PER_STEP = 4
GROUP_WIDTH = HEADS_PER_STEP * HEAD_DIM
N_GROUPS = N_HEADS // HEADS_PER_STEP

LOG2E = 1.4426950408889634
SCORE_LOG2 = SCALE * LOG2E
ZERO_WEIGHT_LOG2 = -160.0
VT_ROWS = HEAD_DIM + 16
MOBA_STEP_BLOCKS = 2


def _head_cols(g):
    return slice(g * HEAD_DIM, (g + 1) * HEAD_DIM)


def _transpose_values(v_ref, vt_ref, g):
    n_chunks = v_ref.shape[0] // ATT_TILE

    def body(c, carry):
        r0 = pl.multiple_of(c * ATT_TILE, ATT_TILE)
        vt_ref[g, :HEAD_DIM, pl.ds(r0, ATT_TILE)] = (
            v_ref[pl.ds(r0, ATT_TILE), _head_cols(g)].astype(F32).T.astype(BF16))
        return carry

    lax.fori_loop(0, n_chunks, body, 0, unroll=8)


def _nt_dot(a, b):
    return lax.dot_general(a, b, (((1,), (1,)), ((), ())), preferred_element_type=F32)


def _neg_abs(x):
    bits = lax.bitcast_convert_type(x, jnp.uint32) | jnp.uint32(0x80000000)
    return lax.bitcast_convert_type(bits, F32)


SB_TILES = 2


def _sb_kernel(q_ref, k_ref, v_ref, o_ref, vt_ref, acc_ref):
    ip = pl.program_id(2)
    t = ATT_TILE
    tasks = [(a, g) for a in range(SB_TILES) for g in range(HEADS_PER_STEP)]

    @pl.when(ip == 0)
    def _():
        for g in range(HEADS_PER_STEP):
            _transpose_values(v_ref, vt_ref, g)

    row = lax.broadcasted_iota(jnp.int32, (t, t), 0)
    col = lax.broadcasted_iota(jnp.int32, (t, t), 1)
    upper = (col > row).astype(BF16)

    def block(blocks, carries, diagonal):
        starts = [pl.multiple_of(jnp.maximum(j, 0) * t, t) for j in blocks]
        zs = [_nt_dot(k_ref[pl.ds(starts[a], t), _head_cols(g)],
                      q_ref[a * t:(a + 1) * t, _head_cols(g)]) for a, g in tasks]
        log_betas, log_1ms, cums = [], [], []
        for n in range(len(tasks)):
            z = zs[n]
            log_beta = jnp.minimum(z, 0.0) - jnp.log2(1.0 + jnp.exp2(_neg_abs(z)))
            log_1m = log_beta - z
            if diagonal:
                log_1m = jnp.where(row < col, log_1m, 0.0)
            hi = log_1m.astype(BF16)
            lo = (log_1m - hi.astype(F32)).astype(BF16)
            cums.append(jnp.dot(upper, hi, preferred_element_type=F32)
                        + jnp.dot(upper, lo, preferred_element_type=F32))
            log_betas.append(log_beta)
            log_1ms.append(log_1m)
        pvs, new_carries = [], []
        for n, (a, g) in enumerate(tasks):
            w = jnp.exp2(log_betas[n] + (cums[n] + carries[n]))
            if diagonal:
                w = jnp.where(row < col, w, 0.0)
            pvs.append(jnp.dot(vt_ref[g, :, pl.ds(starts[a], t)], w.astype(BF16),
                               preferred_element_type=F32))
            carry = carries[n] + jnp.sum(log_1ms[n], axis=0, keepdims=True)
            new_carries.append(jnp.where(blocks[a] <= 0, NEG, carry))
        return new_carries, pvs

    def any_alive(carries):
        top = carries[0]
        for c in carries[1:]:
            top = jnp.maximum(top, c)
        return jnp.max(top, axis=1, keepdims=True)[0, 0] > ZERO_WEIGHT_LOG2

    first = [SB_TILES * ip + a for a in range(SB_TILES)]
    carries, pvs = block(first, [jnp.zeros((1, t), F32) for _ in tasks], True)
    for n in range(len(tasks)):
        acc_ref[n] = pvs[n]

    def cond(state):
        step, alive = state[0], state[1]
        return jnp.logical_and(step < first[-1], alive)

    def body(state):
        step = state[0]
        carries, pvs = block([j - 1 - step for j in first], list(state[2:]), False)
        for n in range(len(tasks)):
            acc_ref[n] += pvs[n]
        return (step + 1, any_alive(carries), *carries)

    lax.while_loop(cond, body, (jnp.int32(0), any_alive(carries), *carries))
    for n, (a, g) in enumerate(tasks):
        o_ref[a * t:(a + 1) * t, _head_cols(g)] = acc_ref[n].T.astype(o_ref.dtype)


def sb_attention(p2, batch, seq):
    t = ATT_TILE
    rows = SB_TILES * t
    nq = seq // rows
    qc, kc, vc = N_GROUPS, 2 * N_GROUPS, 3 * N_GROUPS
    return pl.pallas_call(
        _sb_kernel,
        grid=(batch, N_GROUPS, nq),
        in_specs=[pl.BlockSpec((rows, GROUP_WIDTH), lambda b, h, i: (b * nq + i, qc + h)),
                  pl.BlockSpec((seq, GROUP_WIDTH), lambda b, h, i: (b, kc + h)),
                  pl.BlockSpec((seq, GROUP_WIDTH), lambda b, h, i: (b, vc + h))],
        out_specs=pl.BlockSpec((rows, GROUP_WIDTH), lambda b, h, i: (b * nq + i, h)),
        out_shape=jax.ShapeDtypeStruct((batch * seq, WIDTH), BF16),
        scratch_shapes=[pltpu.VMEM((HEADS_PER_STEP, HEAD_DIM, seq), BF16),
                        pltpu.VMEM((SB_TILES * HEADS_PER_STEP, HEAD_DIM, t), F32)],
        compiler_params=_params(("arbitrary", "arbitrary", "arbitrary")),
        name="sb_attention",
    )(p2, p2, p2)


MOBA_TILES = 2


def _moba_kernel(slopes_ref, inv_span_ref, q_ref, k_ref, v_ref, o_ref, vt_ref, kmh_ref, kml_ref,
                 kbias_ref, selb_ref, acc_ref, knorm_ref, *p_refs, n_blocks):
    hg = pl.program_id(1)
    ip = pl.program_id(2)
    t = ATT_TILE
    heads = range(HEADS_PER_STEP)
    tasks = [(a, g) for a in range(MOBA_TILES) for g in heads]
    n_tasks = len(tasks)
    nb = MOBA_STEP_BLOCKS
    seq = k_ref.shape[0]
    tile_block = [MOBA_TILES * ip + a for a in range(MOBA_TILES)]

    row = lax.broadcasted_iota(jnp.int32, (t, t), 0)
    col = lax.broadcasted_iota(jnp.int32, (t, t), 1)
    slope_log2 = [slopes_ref[hg * HEADS_PER_STEP + g] * LOG2E for g in heads]
    inv_span = [inv_span_ref[hg * HEADS_PER_STEP + g] for g in heads]

    @pl.when(ip == 0)
    def _():
        for g in heads:
            _transpose_values(v_ref, vt_ref, g)
            vt_ref[g, HEAD_DIM:, :] = jnp.ones((VT_ROWS - HEAD_DIM, seq + nb * t), BF16)
            vt_ref[g, :HEAD_DIM, pl.ds(seq, nb * t)] = jnp.zeros((HEAD_DIM, nb * t), BF16)
            k32 = k_ref[:, _head_cols(g)].astype(F32)
            kn2 = jnp.max(jnp.sum(k32 * k32, axis=1, keepdims=True), axis=0, keepdims=True)
            knorm_ref[g] = jnp.broadcast_to(kn2, knorm_ref.shape[1:])
            km = jnp.mean(k32.reshape(n_blocks, t, HEAD_DIM), axis=1)
            hi = km.astype(BF16)
            kmh_ref[g] = hi
            kml_ref[g] = (km - hi.astype(F32)).astype(BF16)
            kbias_ref[g] = row.astype(F32) * slope_log2[g]

    blk = lax.broadcasted_iota(jnp.int32, (n_blocks, t), 0)

    def query(a, g):
        return q_ref[a * t:(a + 1) * t, _head_cols(g)]

    def scores(j, a, g):
        k0 = pl.multiple_of(jnp.minimum(j, n_blocks - 1) * t, t)
        return _nt_dot(k_ref[pl.ds(k0, t), _head_cols(g)], query(a, g))

    def weighted_sum(j, n, slot, alpha):
        g = tasks[n][1]
        total = alpha * acc_ref[n]
        for h in range(nb):
            k0 = pl.multiple_of((j + h) * t, t)
            total = total + jnp.dot(vt_ref[g, :, pl.ds(k0, t)], p_refs[n][slot, h],
                                    preferred_element_type=F32)
        acc_ref[n] = total

    ms, own_ps, keeps = [], [], []
    for n, (a, g) in enumerate(tasks):
        q = query(a, g)
        valid = blk < tile_block[a]
        gate = _nt_dot(kmh_ref[g], q) + _nt_dot(kml_ref[g], q)
        gate = jnp.where(valid, gate, NEG)
        chosen = jnp.zeros((n_blocks, t), jnp.bool_)
        for _ in range(MOBA_TOPK):
            top = jnp.max(gate, axis=0, keepdims=True)
            first = jnp.min(jnp.where(gate == top, blk, n_blocks), axis=0, keepdims=True)
            pick = blk == first
            chosen = jnp.logical_or(chosen, pick)
            gate = jnp.where(pick, -jnp.inf, gate)
        selb_ref[n] = jnp.where(jnp.logical_and(chosen, valid), 0.0, NEG)

        x = jnp.where(col >= row, scores(tile_block[a], a, g) + kbias_ref[g], NEG)
        m = jnp.max(x, axis=0, keepdims=True)
        ms.append(m)
        own_ps.append(jnp.exp2(x - m).astype(BF16))
        acc_ref[n] = jnp.zeros((VT_ROWS, t), F32)

        q32 = q.astype(F32)
        qn2 = jnp.max(jnp.sum(q32 * q32, axis=1, keepdims=True), axis=0, keepdims=True)
        reach = jnp.sqrt(qn2 * knorm_ref[g, :1, :1]) - jnp.min(m, axis=1, keepdims=True)
        keeps.append((reach - ZERO_WEIGHT_LOG2) * inv_span[g] + 1.0)

    keep = keeps[0]
    for other in keeps[1:]:
        keep = jnp.maximum(keep, other)
    keep = jnp.clip(keep, 0.0, float(n_blocks)).astype(jnp.int32)[0, 0]
    first_step = jnp.maximum(tile_block[0] - keep, 0) // nb
    for n in range(n_tasks):
        p_refs[n][first_step & 1, 0] = own_ps[n]
        for h in range(1, nb):
            p_refs[n][first_step & 1, h] = jnp.zeros((t, t), BF16)

    def body(step, state):
        prevs, state = state[:MOBA_TILES], state[MOBA_TILES:]
        alphas, ms = state[:n_tasks], state[n_tasks:]
        slot = step & 1
        j = nb * step
        for n, (a, g) in enumerate(tasks):
            weighted_sum(prevs[a], n, slot, alphas[n])
        dots = [[scores(j + h, a, g) for h in range(nb)] for a, g in tasks]
        new_alphas, new_ms = [], []
        for n, (a, g) in enumerate(tasks):
            base = slope_log2[g] * ((j - tile_block[a]) * t).astype(F32)
            xs = [dots[n][h] + kbias_ref[g] for h in range(nb)]
            shifts = [(base + slope_log2[g] * (h * t))
                      + selb_ref[n, pl.ds(jnp.minimum(j + h, n_blocks - 1), 1), :] for h in range(nb)]
            m_new = ms[n]
            for h in range(nb):
                m_new = jnp.maximum(m_new, jnp.max(xs[h], axis=0, keepdims=True) + shifts[h])
            new_alphas.append(jnp.exp2(ms[n] - m_new))
            for h in range(nb):
                p_refs[n][1 - slot, h] = jnp.exp2(xs[h] - (m_new - shifts[h])).astype(BF16)
            new_ms.append(m_new)
        return (*([j] * MOBA_TILES), *new_alphas, *new_ms)

    n_steps = (tile_block[-1] + nb - 1) // nb
    ones = [jnp.ones((1, t), F32) for _ in tasks]
    state = lax.fori_loop(first_step, n_steps, body, (*tile_block, *ones, *ms))
    prevs, alphas = state[:MOBA_TILES], state[MOBA_TILES:MOBA_TILES + n_tasks]
    for n, (a, g) in enumerate(tasks):
        weighted_sum(prevs[a], n, n_steps & 1, alphas[n])
        acc = acc_ref[n]
        o_ref[a * t:(a + 1) * t, _head_cols(g)] = (
            acc[:HEAD_DIM] / acc[HEAD_DIM:HEAD_DIM + 1]).T.astype(o_ref.dtype)


def moba_attention(qk, p2, slopes, batch, seq):
    t = ATT_TILE
    rows = MOBA_TILES * t
    nq = seq // rows
    n_blocks = seq // MOBA_BLOCK
    n_tasks = MOBA_TILES * HEADS_PER_STEP
    grid_spec = pltpu.PrefetchScalarGridSpec(
        num_scalar_prefetch=2,
        grid=(batch, N_GROUPS, nq),
        in_specs=[pl.BlockSpec((rows, GROUP_WIDTH), lambda b, h, i, s, r: (b * nq + i, h)),
                  pl.BlockSpec((seq, GROUP_WIDTH), lambda b, h, i, s, r: (b, N_GROUPS + h)),
                  pl.BlockSpec((seq, GROUP_WIDTH), lambda b, h, i, s, r: (b, h))],
        out_specs=pl.BlockSpec((rows, GROUP_WIDTH), lambda b, h, i, s, r: (b * nq + i, h)),
        scratch_shapes=[pltpu.VMEM((HEADS_PER_STEP, VT_ROWS, seq + MOBA_STEP_BLOCKS * t), BF16),
                        pltpu.VMEM((HEADS_PER_STEP, n_blocks, HEAD_DIM), BF16),
                        pltpu.VMEM((HEADS_PER_STEP, n_blocks, HEAD_DIM), BF16),
                        pltpu.VMEM((HEADS_PER_STEP, t, t), F32),
                        pltpu.VMEM((n_tasks, n_blocks, t), F32),
                        pltpu.VMEM((n_tasks, VT_ROWS, t), F32),
                        pltpu.VMEM((HEADS_PER_STEP, 8, HEAD_DIM), F32)]
                       + [pltpu.VMEM((2, MOBA_STEP_BLOCKS, t, t), BF16) for _ in range(n_tasks)],
    )
    return pl.pallas_call(
        functools.partial(_moba_kernel, n_blocks=n_blocks),
        grid_spec=grid_spec,
        out_shape=jax.ShapeDtypeStruct((batch * seq, WIDTH), BF16),
        compiler_params=_params(("arbitrary", "arbitrary", "arbitrary")),
        name="moba_attention",
    )(slopes, 1.0 / (slopes * (LOG2E * MOBA_BLOCK)), qk, qk, p2)


def _layer(x, layer, w_in, weights, vectors, slopes, batch, seq):
    g_mix, b_gate, g_q, g_k, g_mlp = vectors
    depth = weights["w_in"].shape[0]
    h = rms_norm(x, g_mix)
    qk_gain = jnp.concatenate([jnp.tile(g_q * SCORE_LOG2, N_HEADS),
                               jnp.tile(g_k, N_HEADS)]).reshape(1, 2 * WIDTH)
    qk, w_out, w_bm, w_bs = matmul(
        _mm_headnorm_kernel, h, w_in, [(qk_gain, "row")], n=2 * WIDTH, col_off=0,
        out_dtype=BF16, tm=1024, tn=1024, name="in_proj_qk_moba",
        side_casts=[(weights["w_out"], layer), (weights["w_branch_moba"], layer),
                    (weights["w_branch_sb"], layer)])
    rest_scale = jnp.concatenate([jnp.ones((WIDTH,), F32), jnp.full((WIDTH,), SCORE_LOG2, F32),
                                  jnp.ones((2 * WIDTH,), F32)]).reshape(1, 4 * WIDTH)
    p2, w_up = matmul(_mm_colscale_kernel, h, w_in, [(rest_scale, "row")], n=4 * WIDTH,
                      col_off=2 * WIDTH, out_dtype=BF16, tm=1024, tn=1024, name="in_proj_rest",
                      side_casts=[(weights["w_up"], layer)])
    gates = matmul(_mm_sigmoid_kernel, h, w_in, [(b_gate.reshape(1, -1), "row")], n=2 * D_MODEL,
                   col_off=6 * WIDTH, out_dtype=F32, tm=1024, tn=1024, name="in_proj_gates")
    ya = moba_attention(qk, p2, slopes, batch, seq)
    yb = sb_attention(p2, batch, seq)
    merged = merge_branches(ya, yb, w_bm, w_bs, gates)
    x = matmul(_mm_residual_kernel, merged, w_out, [(x, "tile")], n=D_MODEL, col_off=0,
               out_dtype=F32, tm=1024, tn=1024, name="out_proj")
    h2 = rms_norm(x, g_mlp)
    up_casts = [(weights["w_down"], layer)]
    if layer + 1 < depth:
        up_casts.append((weights["w_in"], layer + 1))
    u, w_down, *next_w_in = matmul(_mm_relu2_kernel, h2, w_up, [], n=D_FF, col_off=0,
                                   out_dtype=BF16, tm=1024, tn=1024, name="mlp_up",
                                   side_casts=up_casts)
    x = matmul_kgrid_residual(u, w_down, x, tm=1024, tn=1024, tk=2048, name="mlp_down")
    return x, (next_w_in[0] if next_w_in else None)


def kernel(x, norm_mix, w_in, b_gate, q_norm, k_norm, w_branch_moba, w_branch_sb, w_out,
           norm_mlp, w_up, w_down):
    batch, seq, d = x.shape
    depth = w_in.shape[0]
    slopes = jnp.exp2(-8.0 * jnp.arange(1, N_HEADS + 1, dtype=F32) / N_HEADS)
    weights = dict(w_in=w_in, w_branch_moba=w_branch_moba, w_branch_sb=w_branch_sb, w_out=w_out,
                   w_up=w_up, w_down=w_down)
    y = x.reshape(batch * seq, d)
    layer_w_in = layer_weight_bf16(w_in, 0)
    for l in range(depth):
        vectors = (norm_mix[l], b_gate[l], q_norm[l], k_norm[l], norm_mlp[l])
        y, layer_w_in = _layer(y, l, layer_w_in, weights, vectors, slopes, batch, seq)
    return y.reshape(batch, seq, d)
```
